```python
import math
import jax, jax.numpy as jnp
from jax import lax
import numpy as np


D_MODEL = 2048
BATCH = 4
SEQ = 2048
DEPTH = 1

D_MIX = D_MODEL
ATTN_WIDTH = D_MIX // 2
SSD_WIDTH = D_MIX - ATTN_WIDTH
ATTN_HEADS = 8
ATTN_DH = ATTN_WIDTH // (2 * ATTN_HEADS)
ATTN_DV = 2 * ATTN_DH
ROT_DIM = ATTN_DH // 4
ROPE_THETA = 500000.0
Q_BLOCK = 128
SSD_HEADDIM = 64
SSD_HEADS = SSD_WIDTH // SSD_HEADDIM
SSD_GROUPS = 2
SSD_STATE = 128
CONV_W = 4
CONV_DIM = SSD_WIDTH + 2 * SSD_GROUPS * SSD_STATE
CHUNK = 128
N_EXPERTS = 32
TOP_K = 4
D_FF = D_MODEL
SWIGLU_ALPHA = 1.702
SWIGLU_LIMIT = 7.0
EPS = 1e-5
IN_SIZES = (ATTN_WIDTH, ATTN_WIDTH, ATTN_HEADS * ATTN_DV, SSD_WIDTH, CONV_DIM, SSD_HEADS)
IN_TOTAL = sum(IN_SIZES)

kernel_name = "hybrid_diffattn_ssd_moe_layer"


def rms_norm(x, w):
    xf = x.astype(jnp.float32)
    xf = xf * lax.rsqrt(jnp.mean(xf * xf, axis=-1, keepdims=True) + EPS)
    return (xf * w.astype(jnp.float32)).astype(x.dtype)


def lambda_init(layer):
    return 0.8 - 0.6 * math.exp(-0.3 * layer)


def rope_tables(positions, dtype):
    inv_freq = ROPE_THETA ** (-jnp.arange(0, ROT_DIM, 2, dtype=jnp.float32) / ROT_DIM)
    ang = positions.astype(jnp.float32)[..., None] * inv_freq
    cos = jnp.cos(ang)[:, :, None, None, :].astype(dtype)
    sin = jnp.sin(ang)[:, :, None, None, :].astype(dtype)
    return cos, sin


def rope_partial(t, cos, sin):
    half = ROT_DIM // 2
    t1 = t[..., :half]
    t2 = t[..., half:ROT_DIM]
    return jnp.concatenate([t1 * cos - t2 * sin, t2 * cos + t1 * sin, t[..., ROT_DIM:]], axis=-1)


def diff_attention(q, k, v, lam):
    s = q.shape[1]
    scale = ATTN_DH ** -0.5
    outs = []
    for i0 in range(0, s, Q_BLOCK):
        i1 = i0 + Q_BLOCK
        sc = jnp.einsum('bqhcd,bkhcd->bhcqk', q[:, i0:i1], k[:, :i1]).astype(jnp.float32) * scale
        causal = jnp.arange(i1)[None, :] <= jnp.arange(i0, i1)[:, None]
        p = jax.nn.softmax(jnp.where(causal, sc, -jnp.inf), axis=-1)
        a = p[:, :, 0] - lam * p[:, :, 1]
        outs.append(jnp.einsum('bhqk,bkhe->bqhe', a.astype(v.dtype), v[:, :i1]))
    return jnp.concatenate(outs, axis=1)


def causal_dwconv(u, w, b):
    ch = u.shape[-1]
    y = lax.conv_general_dilated(u, w[:, None, :].astype(u.dtype), window_strides=(1,),
                                 padding=[(CONV_W - 1, 0)],
                                 dimension_numbers=('NWC', 'WIO', 'NWC'),
                                 feature_group_count=ch)
    return y + b.astype(u.dtype)


def ssd_chunked(x, dt, a, bmat, cmat):
    bsz, l, h, p = x.shape
    g, n = bmat.shape[2], bmat.shape[3]
    r = h // g
    c = l // CHUNK
    xdt = (x.astype(jnp.float32) * dt[..., None]).reshape(bsz, c, CHUNK, g, r, p)
    bc = bmat.astype(jnp.float32).reshape(bsz, c, CHUNK, g, n)
    cc = cmat.astype(jnp.float32).reshape(bsz, c, CHUNK, g, n)
    da = (dt * a).reshape(bsz, c, CHUNK, g, r).transpose(0, 3, 4, 1, 2)
    cs = jnp.cumsum(da, axis=-1)
    causal = jnp.tril(jnp.ones((CHUNK, CHUNK), dtype=bool))
    seg = cs[..., :, None] - cs[..., None, :]
    decay = jnp.exp(jnp.where(causal, seg, -jnp.inf))
    cb = jnp.einsum('bclgn,bcsgn->bgcls', cc, bc)
    y_diag = jnp.einsum('bgrcls,bcsgrp->bclgrp', cb[:, :, None] * decay, xdt)
    decay_states = jnp.exp(cs[..., -1:] - cs)
    states = jnp.einsum('bcsgn,bgrcs,bcsgrp->bcgrpn', bc, decay_states, xdt)
    chunk_decay = jnp.exp(cs[..., -1])

    def step(hs, inp):
        s_c, d_c = inp
        return d_c[..., None, None] * hs + s_c, hs

    init = jnp.zeros((bsz, g, r, p, n), jnp.float32)
    _, prev = lax.scan(step, init, (jnp.moveaxis(states, 1, 0), jnp.moveaxis(chunk_decay, 3, 0)))
    prev = jnp.moveaxis(prev, 0, 1)
    y_off = jnp.einsum('bclgn,bcgrpn,bgrcl->bclgrp', cc, prev, jnp.exp(cs))
    return (y_diag + y_off).reshape(bsz, l, h, p)


def moe_ffn(h, router_w, router_b, w_gate_up, b_gate_up, w_down, b_down):
    bsz, s, d = h.shape
    t = h.reshape(bsz * s, d)
    logits = (t @ router_w + router_b).astype(jnp.float32)
    top_vals, top_idx = lax.top_k(logits, TOP_K)
    gates = jax.nn.softmax(top_vals, axis=-1)
    combine = jnp.einsum('tk,tke->te', gates, jax.nn.one_hot(top_idx, N_EXPERTS, dtype=jnp.float32))
    combine = combine.astype(h.dtype)
    out = jnp.zeros_like(t)
    for e in range(N_EXPERTS):
        gu = t @ w_gate_up[e] + b_gate_up[e]
        gate = jnp.minimum(gu[:, :D_FF], SWIGLU_LIMIT)
        up = jnp.clip(gu[:, D_FF:], -SWIGLU_LIMIT, SWIGLU_LIMIT)
        act = gate * jax.nn.sigmoid(SWIGLU_ALPHA * gate) * (up + 1.0)
        out = out + combine[:, e:e + 1] * (act @ w_down[e] + b_down[e])
    return out.reshape(bsz, s, d)


def setup_inputs(seed: int = 0) -> dict:
    key = jax.random.key(seed)
    ks = jax.random.split(key, 24)
    f32 = jnp.float32
    nrm = lambda k, shape, scale: jax.random.normal(k, shape, f32) * scale
    x = jax.random.normal(ks[0], (BATCH, SEQ, D_MODEL), f32)
    offsets = jax.random.randint(ks[1], (BATCH, 1), 0, 4096, dtype=jnp.int32)
    positions = offsets + jnp.arange(SEQ, dtype=jnp.int32)[None, :]
    dt0 = jnp.exp(jax.random.uniform(ks[10], (DEPTH, SSD_HEADS), f32)
                  * (math.log(0.1) - math.log(0.001)) + math.log(0.001))
    return {
        "x": x,
        "positions": positions,
        "norm1_w": 1.0 + nrm(ks[2], (DEPTH, D_MODEL), 0.02),
        "w_in": nrm(ks[3], (DEPTH, D_MODEL, IN_TOTAL), D_MODEL ** -0.5),
        "lambda_q1": nrm(ks[4], (DEPTH, ATTN_DH), 0.1),
        "lambda_k1": nrm(ks[5], (DEPTH, ATTN_DH), 0.1),
        "lambda_q2": nrm(ks[6], (DEPTH, ATTN_DH), 0.1),
        "lambda_k2": nrm(ks[7], (DEPTH, ATTN_DH), 0.1),
        "attn_subln_w": 1.0 + nrm(ks[8], (DEPTH, ATTN_DV), 0.02),
        "conv_w": nrm(ks[9], (DEPTH, CONV_W, CONV_DIM), CONV_W ** -0.5),
        "conv_b": nrm(ks[11], (DEPTH, CONV_DIM), 0.02),
        "dt_bias": dt0 + jnp.log(-jnp.expm1(-dt0)),
        "a_log": jnp.log(jax.random.uniform(ks[12], (DEPTH, SSD_HEADS), f32, 1.0, 16.0)),
        "d_skip": 1.0 + nrm(ks[13], (DEPTH, SSD_HEADS), 0.02),
        "ssd_norm_w": 1.0 + nrm(ks[14], (DEPTH, SSD_WIDTH), 0.02),
        "w_out": nrm(ks[15], (DEPTH, D_MIX, D_MODEL), D_MIX ** -0.5),
        "norm2_w": 1.0 + nrm(ks[16], (DEPTH, D_MODEL), 0.02),
        "router_w": nrm(ks[17], (DEPTH, D_MODEL, N_EXPERTS), D_MODEL ** -0.5),
        "router_b": nrm(ks[18], (DEPTH, N_EXPERTS), 0.01),
        "w_gate_up": nrm(ks[19], (DEPTH, N_EXPERTS, D_MODEL, 2 * D_FF), D_MODEL ** -0.5),
        "b_gate_up": nrm(ks[20], (DEPTH, N_EXPERTS, 2 * D_FF), 0.02),
        "w_down": nrm(ks[21], (DEPTH, N_EXPERTS, D_FF, D_MODEL), D_FF ** -0.5),
        "b_down": nrm(ks[22], (DEPTH, N_EXPERTS, D_MODEL), 0.02),
        "norm_f_w": 1.0 + nrm(ks[23], (D_MODEL,), 0.02),
    }


def reference(x, positions, norm1_w, w_in, lambda_q1, lambda_k1, lambda_q2, lambda_k2,
              attn_subln_w, conv_w, conv_b, dt_bias, a_log, d_skip, ssd_norm_w, w_out,
              norm2_w, router_w, router_b, w_gate_up, b_gate_up, w_down, b_down, norm_f_w):
    bsz, s, _ = x.shape
    cos, sin = rope_tables(positions, x.dtype)
    offs = np.cumsum(IN_SIZES)[:-1].tolist()
    for layer in range(DEPTH):
        h = rms_norm(x, norm1_w[layer])
        proj = h @ w_in[layer]
        q, k, v, z, xbc, dt_raw = jnp.split(proj, offs, axis=-1)

        q = rope_partial(q.reshape(bsz, s, ATTN_HEADS, 2, ATTN_DH), cos, sin)
        k = rope_partial(k.reshape(bsz, s, ATTN_HEADS, 2, ATTN_DH), cos, sin)
        v = v.reshape(bsz, s, ATTN_HEADS, ATTN_DV)
        lam_init = lambda_init(layer)
        lam = (jnp.exp(jnp.sum(lambda_q1[layer].astype(jnp.float32) * lambda_k1[layer].astype(jnp.float32)))
               - jnp.exp(jnp.sum(lambda_q2[layer].astype(jnp.float32) * lambda_k2[layer].astype(jnp.float32)))
               + lam_init)
        attn = diff_attention(q, k, v, lam)
        attn = rms_norm(attn, attn_subln_w[layer]) * (1.0 - lam_init)
        attn = attn.reshape(bsz, s, ATTN_HEADS * ATTN_DV)

        xbc = jax.nn.silu(causal_dwconv(xbc, conv_w[layer], conv_b[layer]))
        xs, bmat, cmat = jnp.split(xbc, [SSD_WIDTH, SSD_WIDTH + SSD_GROUPS * SSD_STATE], axis=-1)
        xs = xs.reshape(bsz, s, SSD_HEADS, SSD_HEADDIM)
        bmat = bmat.reshape(bsz, s, SSD_GROUPS, SSD_STATE)
        cmat = cmat.reshape(bsz, s, SSD_GROUPS, SSD_STATE)
        dt = jax.nn.softplus(dt_raw.astype(jnp.float32) + dt_bias[layer].astype(jnp.float32))
        a = -jnp.exp(a_log[layer].astype(jnp.float32))
        y = ssd_chunked(xs, dt, a, bmat, cmat)
        y = y + d_skip[layer].astype(jnp.float32)[:, None] * xs.astype(jnp.float32)
        y = y.reshape(bsz, s, SSD_WIDTH) * jax.nn.silu(z.astype(jnp.float32))
        yg = y.reshape(bsz, s, SSD_GROUPS, SSD_WIDTH // SSD_GROUPS)
        yg = yg * lax.rsqrt(jnp.mean(yg * yg, axis=-1, keepdims=True) + EPS)
        ssd = (yg.reshape(bsz, s, SSD_WIDTH) * ssd_norm_w[layer].astype(jnp.float32)).astype(x.dtype)

        x = x + jnp.concatenate([attn, ssd], axis=-1) @ w_out[layer]

        x = x + moe_ffn(rms_norm(x, norm2_w[layer]), router_w[layer], router_b[layer],
                        w_gate_up[layer], b_gate_up[layer], w_down[layer], b_down[layer])
    return rms_norm(x, norm_f_w)
```

```python
import functools
import math

import jax
import jax.numpy as jnp
from jax import lax
from jax.experimental import pallas as pl
from jax.experimental.pallas import tpu as pltpu

F32 = jnp.float32
BF16 = jnp.bfloat16
I32 = jnp.int32

D_MODEL = 2048
ATTN_WIDTH = 1024
SSD_WIDTH = 1024
ATTN_HEADS = 8
ATTN_DH = 64
ATTN_DV = 128
ROT_DIM = 16
ROT_HALF = ROT_DIM // 2
ROPE_THETA = 500000.0
SSD_HEADDIM = 64
SSD_HEADS = 16
SSD_GROUPS = 2
SSD_STATE = 128
CONV_W = 4
CONV_DIM = SSD_WIDTH + 2 * SSD_GROUPS * SSD_STATE
CHUNK = 128
N_EXPERTS = 32
TOP_K = 4
D_FF = 2048
SWIGLU_ALPHA = 1.702
SWIGLU_LIMIT = 7.0
EPS = 1e-5
IN_TOTAL = 3 * ATTN_WIDTH + SSD_WIDTH + CONV_DIM + SSD_HEADS
LAMBDA_INIT = 0.8 - 0.6 * math.exp(-0.3 * 0)

LANES = 128
SUBLANES = 8
VMEM_LIMIT = 56 * 1024 * 1024

OFF_Q = 0
OFF_K = ATTN_WIDTH
OFF_V = 2 * ATTN_WIDTH
OFF_Z = 3 * ATTN_WIDTH
OFF_X = OFF_Z + SSD_WIDTH
OFF_B = OFF_X + SSD_WIDTH
OFF_C = OFF_B + SSD_GROUPS * SSD_STATE
OFF_DT = OFF_C + SSD_GROUPS * SSD_STATE


def _sigmoid(x):
    return 1.0 / (1.0 + jnp.exp(-x))


def _softplus(x):
    return jnp.maximum(x, 0.0) + jnp.log1p(jnp.exp(-jnp.abs(x)))


def _in_proj_kernel(pos_ref, invf_ref, x_ref, nw_ref, w_ref, o_ref,
                    hn_ref, c_ref, sa_ref, sb_ref, *, tn, n_q_tiles, n_rope_tiles):
    j = pl.program_id(1)

    @pl.when(j == 0)
    def _prepare():
        x = x_ref[...]
        ms = jnp.mean(x * x, axis=-1, keepdims=True)
        hn_ref[...] = (x * lax.rsqrt(ms + EPS) * nw_ref[...]).astype(BF16)
        ang = pos_ref[...].astype(F32) * invf_ref[...]
        d = lax.broadcasted_iota(I32, ang.shape, 1) % ATTN_DH
        cos = jnp.cos(ang)
        sin = jnp.sin(ang)
        c_ref[...] = jnp.where(d < ROT_DIM, cos, 1.0)
        sa_ref[...] = jnp.where(d < ROT_HALF, 0.0, jnp.where(d < ROT_DIM, sin, 0.0))
        sb_ref[...] = jnp.where(d < ROT_HALF, -sin, 0.0)

    acc = jnp.dot(hn_ref[...], w_ref[...].astype(BF16), preferred_element_type=F32)

    @pl.when(j < n_rope_tiles)
    def _rope():
        reps = tn // LANES
        c = jnp.tile(c_ref[...], (1, reps))
        sa = jnp.tile(sa_ref[...], (1, reps))
        sb = jnp.tile(sb_ref[...], (1, reps))
        r = acc * c + pltpu.roll(acc, ROT_HALF, 1) * sa + pltpu.roll(acc, tn - ROT_HALF, 1) * sb
        scale = jnp.where(j < n_q_tiles, ATTN_DH ** -0.5, 1.0)
        o_ref[...] = r * scale

    @pl.when(j >= n_rope_tiles)
    def _plain():
        o_ref[...] = acc


def _in_proj(x2d, pos2d, invf, norm_w, w_in, *, tm, tn):
    t = x2d.shape[0]
    n = w_in.shape[1]
    grid = (t // tm, pl.cdiv(n, tn))
    kern = functools.partial(_in_proj_kernel, tn=tn, n_q_tiles=ATTN_WIDTH // tn,
                             n_rope_tiles=2 * ATTN_WIDTH // tn)
    return pl.pallas_call(
        kern,
        grid=grid,
        in_specs=[
            pl.BlockSpec((tm, 1), lambda i, j: (i, 0)),
            pl.BlockSpec((1, LANES), lambda i, j: (0, 0)),
            pl.BlockSpec((tm, D_MODEL), lambda i, j: (i, 0)),
            pl.BlockSpec((1, D_MODEL), lambda i, j: (0, 0)),
            pl.BlockSpec((D_MODEL, tn), lambda i, j: (0, j)),
        ],
        out_specs=pl.BlockSpec((tm, tn), lambda i, j: (i, j)),
        out_shape=jax.ShapeDtypeStruct((t, n), F32),
        scratch_shapes=[
            pltpu.VMEM((tm, D_MODEL), BF16),
            pltpu.VMEM((tm, LANES), F32),
            pltpu.VMEM((tm, LANES), F32),
            pltpu.VMEM((tm, LANES), F32),
        ],
        compiler_params=pltpu.CompilerParams(
            dimension_semantics=("parallel", "arbitrary"), vmem_limit_bytes=VMEM_LIMIT),
        name="in_proj",
    )(pos2d, invf, x2d, norm_w, w_in)


def _attn_kernel(lq1_ref, lk1_ref, lq2_ref, lk2_ref, q_ref, k_ref, v_ref, w_ref, o_ref, *, tq):
    i = pl.program_id(2)
    q = q_ref[...]
    lane = lax.broadcasted_iota(I32, q.shape, 1)
    q0 = jnp.where(lane < ATTN_DH, q, 0.0).astype(BF16)
    q1 = jnp.where(lane < ATTN_DH, 0.0, q).astype(BF16)
    qq = jnp.concatenate([q0, q1], axis=0)

    def block(n, carry, masked):
        m, l, acc = carry
        off = pl.multiple_of(n * tq, tq)
        kb = k_ref[pl.ds(off, tq), :].astype(BF16)
        vb = v_ref[pl.ds(off, tq), :].astype(BF16)
        s = lax.dot_general(qq, kb, (((1,), (1,)), ((), ())), preferred_element_type=F32)
        if masked:
            r = lax.broadcasted_iota(I32, s.shape, 0) % tq
            c = lax.broadcasted_iota(I32, s.shape, 1)
            s = jnp.where(c <= r, s, -jnp.inf)
        m_new = jnp.maximum(m, jnp.max(s, axis=-1, keepdims=True))
        a = jnp.exp(m - m_new)
        p = jnp.exp(s - m_new)
        l = a * l + jnp.sum(p, axis=-1, keepdims=True)
        acc = a * acc + jnp.dot(p.astype(BF16), vb, preferred_element_type=F32)
        return m_new, l, acc

    init = (jnp.full((2 * tq, 1), -jnp.inf, F32), jnp.zeros((2 * tq, 1), F32),
            jnp.zeros((2 * tq, ATTN_DV), F32))
    carry = lax.fori_loop(0, i, lambda n, c: block(n, c, False), init)
    _, l, acc = block(i, carry, True)

    o = acc / l
    lam = (jnp.exp(jnp.sum(lq1_ref[...] * lk1_ref[...], axis=-1, keepdims=True))
           - jnp.exp(jnp.sum(lq2_ref[...] * lk2_ref[...], axis=-1, keepdims=True)) + LAMBDA_INIT)
    d = o[:tq] - lam * o[tq:]
    ms = jnp.mean(d * d, axis=-1, keepdims=True)
    o_ref[...] = ((d * lax.rsqrt(ms + EPS) * w_ref[...]) * (1.0 - LAMBDA_INIT)).astype(o_ref.dtype)


def _attention(proj, lq1, lk1, lq2, lk2, subln_w, *, bsz, seq, tq):
    nq = seq // tq
    kb0 = OFF_K // ATTN_DV
    vb0 = OFF_V // ATTN_DV
    vec = pl.BlockSpec((1, ATTN_DH), lambda b, h, i: (0, 0))
    return pl.pallas_call(
        functools.partial(_attn_kernel, tq=tq),
        grid=(bsz, ATTN_HEADS, nq),
        in_specs=[
            vec, vec, vec, vec,
            pl.BlockSpec((tq, ATTN_DV), lambda b, h, i: (b * nq + i, h)),
            pl.BlockSpec((seq, ATTN_DV), lambda b, h, i: (b, kb0 + h)),
            pl.BlockSpec((seq, ATTN_DV), lambda b, h, i: (b, vb0 + h)),
            pl.BlockSpec((1, ATTN_DV), lambda b, h, i: (0, 0)),
        ],
        out_specs=pl.BlockSpec((tq, ATTN_DV), lambda b, h, i: (b * nq + i, h)),
        out_shape=jax.ShapeDtypeStruct((bsz * seq, ATTN_WIDTH), BF16),
        compiler_params=pltpu.CompilerParams(
            dimension_semantics=("parallel", "parallel", "arbitrary"), vmem_limit_bytes=VMEM_LIMIT),
        name="diff_attention",
    )(lq1, lk1, lq2, lk2, proj, proj, proj, subln_w)


def _ssd_kernel(z_ref, xs_ref, b_ref, c_ref, dt_ref, cw_ref, cb_ref, dtb_ref, alog_ref,
                dskip_ref, nw_ref, e_ref, o_ref, ext_ref, state_ref):
    ci = pl.program_id(1)
    q = CHUNK
    gw = SSD_WIDTH // SSD_GROUPS

    @pl.when(ci == 0)
    def _init():
        ext_ref[0:SUBLANES, :] = jnp.zeros((SUBLANES, CONV_DIM), F32)
        state_ref[...] = jnp.zeros_like(state_ref)

    u = jnp.concatenate([xs_ref[...], b_ref[...], c_ref[...]], axis=1)
    ext_ref[SUBLANES:SUBLANES + q, :] = u
    acc = jnp.broadcast_to(cb_ref[...], (q, CONV_DIM))
    for w in range(CONV_W):
        acc = acc + ext_ref[pl.ds(SUBLANES - (CONV_W - 1) + w, q), :] * cw_ref[w:w + 1, :]
    ext_ref[0:SUBLANES, :] = u[q - SUBLANES:q, :]
    xbc = acc * _sigmoid(acc)
    xc = xbc[:, :SSD_WIDTH]
    bm = xbc[:, SSD_WIDTH:SSD_WIDTH + SSD_GROUPS * SSD_STATE]
    cm = xbc[:, SSD_WIDTH + SSD_GROUPS * SSD_STATE:]

    lane = lax.broadcasted_iota(I32, (q, LANES), 1)
    head_ok = lane < SSD_HEADS
    dt = jnp.where(head_ok, _softplus(jnp.where(head_ok, dt_ref[...], 0.0) + dtb_ref[...]), 0.0)
    da = dt * (-jnp.exp(alog_ref[...]))
    row = lax.broadcasted_iota(I32, (q, q), 0)
    col = lax.broadcasted_iota(I32, (q, q), 1)
    causal = row >= col
    cs = jnp.dot(causal.astype(F32), da, preferred_element_type=F32,
                 precision=lax.Precision.HIGHEST)
    cs_t = cs.T
    cs_last = cs[q - 1:q, :]

    stk = jnp.concatenate([dt, jnp.exp(cs), jnp.exp(cs_last - cs)], axis=0)
    hi = stk.astype(BF16)
    lo = (stk - hi.astype(F32)).astype(BF16)
    ex = (jnp.dot(hi, e_ref[...], preferred_element_type=F32)
          + jnp.dot(lo, e_ref[...], preferred_element_type=F32))
    dtx = ex[0:q]
    ecx = ex[q:2 * q]
    dsx = ex[2 * q:3 * q]

    xdt = xc * dtx
    xdt_b = xdt.astype(BF16)
    xw_b = (xdt * dsx).astype(BF16)
    lane_q = lax.broadcasted_iota(I32, (q, LANES), 1)

    y_groups = []
    for g in range(SSD_GROUPS):
        gs = slice(g * gw, (g + 1) * gw)
        bg = bm[:, g * SSD_STATE:(g + 1) * SSD_STATE]
        cg_b = cm[:, g * SSD_STATE:(g + 1) * SSD_STATE].astype(BF16)
        cb = lax.dot_general(cg_b, bg.astype(BF16), (((1,), (1,)), ((), ())),
                             preferred_element_type=F32)
        prev_t = state_ref[g]
        y_off = jnp.dot(cg_b, prev_t.astype(BF16), preferred_element_type=F32) * ecx[:, gs]
        st_t = jnp.dot(bg.T.astype(BF16), xw_b[:, gs], preferred_element_type=F32)
        state_ref[g] = prev_t * ecx[q - 1:q, gs] + st_t

        pairs = []
        for jp in range(gw // LANES):
            rhs = xdt_b[:, g * gw + jp * LANES: g * gw + (jp + 1) * LANES]
            outs = []
            for hh in range(2):
                h = g * (SSD_HEADS // SSD_GROUPS) + jp * 2 + hh
                seg = (jnp.broadcast_to(cs[:, h:h + 1], (q, q))
                       - jnp.broadcast_to(cs_t[h:h + 1, :], (q, q)))
                dec = jnp.exp(jnp.where(causal, seg, -jnp.inf))
                outs.append(jnp.dot((cb * dec).astype(BF16), rhs, preferred_element_type=F32))
            pairs.append(jnp.where(lane_q < SSD_HEADDIM, outs[0], outs[1]))
        y_groups.append(jnp.concatenate(pairs, axis=1) + y_off)

    y = jnp.concatenate(y_groups, axis=1) + dskip_ref[...] * xc
    z = z_ref[...]
    y = y * (z * _sigmoid(z))
    outs = []
    for g in range(SSD_GROUPS):
        yg = y[:, g * gw:(g + 1) * gw]
        outs.append(yg * lax.rsqrt(jnp.mean(yg * yg, axis=-1, keepdims=True) + EPS))
    o_ref[...] = (jnp.concatenate(outs, axis=1) * nw_ref[...]).astype(o_ref.dtype)


def _ssd(proj, conv_w, conv_b, dt_bias_p, a_log_p, dskip_x, ssd_norm_w, expand, *, bsz, seq):
    nc = seq // CHUNK
    sw = SSD_GROUPS * SSD_STATE
    const = lambda shape: pl.BlockSpec(shape, lambda b, c: (0, 0))
    return pl.pallas_call(
        _ssd_kernel,
        grid=(bsz, nc),
        in_specs=[
            pl.BlockSpec((CHUNK, SSD_WIDTH), lambda b, c: (b * nc + c, OFF_Z // SSD_WIDTH)),
            pl.BlockSpec((CHUNK, SSD_WIDTH), lambda b, c: (b * nc + c, OFF_X // SSD_WIDTH)),
            pl.BlockSpec((CHUNK, sw), lambda b, c: (b * nc + c, OFF_B // sw)),
            pl.BlockSpec((CHUNK, sw), lambda b, c: (b * nc + c, OFF_C // sw)),
            pl.BlockSpec((CHUNK, LANES), lambda b, c: (b * nc + c, OFF_DT // LANES)),
            const((CONV_W, CONV_DIM)),
            const((1, CONV_DIM)),
            const((1, LANES)),
            const((1, LANES)),
            const((1, SSD_WIDTH)),
            const((1, SSD_WIDTH)),
            const((LANES, SSD_WIDTH)),
        ],
        out_specs=pl.BlockSpec((CHUNK, SSD_WIDTH), lambda b, c: (b * nc + c, 0)),
        out_shape=jax.ShapeDtypeStruct((bsz * seq, SSD_WIDTH), BF16),
        scratch_shapes=[
            pltpu.VMEM((SUBLANES + CHUNK, CONV_DIM), F32),
            pltpu.VMEM((SSD_GROUPS, SSD_STATE, SSD_WIDTH // SSD_GROUPS), F32),
        ],
        compiler_params=pltpu.CompilerParams(
            dimension_semantics=("parallel", "arbitrary"), vmem_limit_bytes=VMEM_LIMIT),
        name="ssd",
    )(proj, proj, proj, proj, proj, conv_w, conv_b, dt_bias_p, a_log_p, dskip_x, ssd_norm_w, expand)


def _out_proj_kernel(x_ref, a_ref, s_ref, wo_ref, n2_ref, rw_ref, rb_ref, x2_ref, h2_ref, lg_ref):
    x2 = (x_ref[...]
          + jnp.dot(a_ref[...], wo_ref[0:ATTN_WIDTH, :], preferred_element_type=F32)
          + jnp.dot(s_ref[...], wo_ref[ATTN_WIDTH:, :], preferred_element_type=F32))
    x2_ref[...] = x2
    ms = jnp.mean(x2 * x2, axis=-1, keepdims=True)
    h2 = x2 * lax.rsqrt(ms + EPS) * n2_ref[...]
    h2_ref[...] = h2
    lg_ref[...] = lax.dot_general(rw_ref[...], h2, (((1,), (1,)), ((), ())),
                                  preferred_element_type=F32,
                                  precision=lax.Precision.HIGHEST) + rb_ref[...]


def _out_proj(x2d, attn, ssd, w_out_b, norm2_w, router_wt, router_b, *, tm):
    t = x2d.shape[0]
    const = lambda shape: pl.BlockSpec(shape, lambda i: (0, 0))
    return pl.pallas_call(
        _out_proj_kernel,
        grid=(t // tm,),
        in_specs=[
            pl.BlockSpec((tm, D_MODEL), lambda i: (i, 0)),
            pl.BlockSpec((tm, ATTN_WIDTH), lambda i: (i, 0)),
            pl.BlockSpec((tm, SSD_WIDTH), lambda i: (i, 0)),
            const((D_MODEL, D_MODEL)),
            const((1, D_MODEL)),
            const((N_EXPERTS, D_MODEL)),
            const((N_EXPERTS, 1)),
        ],
        out_specs=[
            pl.BlockSpec((tm, D_MODEL), lambda i: (i, 0)),
            pl.BlockSpec((tm, D_MODEL), lambda i: (i, 0)),
            pl.BlockSpec((N_EXPERTS, tm), lambda i: (0, i)),
        ],
        out_shape=[
            jax.ShapeDtypeStruct((t, D_MODEL), F32),
            jax.ShapeDtypeStruct((t, D_MODEL), F32),
            jax.ShapeDtypeStruct((N_EXPERTS, t), F32),
        ],
        compiler_params=pltpu.CompilerParams(
            dimension_semantics=("parallel",), vmem_limit_bytes=VMEM_LIMIT),
        name="out_proj",
    )(x2d, attn, ssd, w_out_b, norm2_w, router_wt, router_b)


def _route_kernel(lg_ref, idx_ref, gate_ref, pos_ref, cnt_ref, off_ref, rank_ref, *, tt, row_align):
    t_total = lg_ref.shape[1]
    nt = t_total // tt
    eio = lax.broadcasted_iota(I32, (N_EXPERTS, tt), 0)
    r = lax.broadcasted_iota(I32, (tt, tt), 0)
    c = lax.broadcasted_iota(I32, (tt, tt), 1)
    before = jnp.where(r < c, 1.0, 0.0).astype(BF16)

    def tile(ti, cnt):
        off = pl.multiple_of(ti * tt, tt)
        work = lg_ref[:, pl.ds(off, tt)]
        vals, hots = [], []
        for k in range(TOP_K):
            m = jnp.max(work, axis=0, keepdims=True)
            ik = jnp.min(jnp.where(work == m, eio, N_EXPERTS), axis=0, keepdims=True)
            hot = eio == ik
            work = jnp.where(hot, -jnp.inf, work)
            idx_ref[k:k + 1, pl.ds(off, tt)] = ik
            vals.append(m)
            hots.append(hot)
        es = [jnp.exp(v - vals[0]) for v in vals]
        inv = 1.0 / (es[0] + es[1] + es[2] + es[3])
        sel = jnp.zeros((N_EXPERTS, tt), F32)
        for k in range(TOP_K):
            gate_ref[k:k + 1, pl.ds(off, tt)] = es[k] * inv
            sel = sel + jnp.where(hots[k], 1.0, 0.0)
        rank = jnp.dot(sel.astype(BF16), before, preferred_element_type=F32) + cnt
        for k in range(TOP_K):
            rank_ref[k:k + 1, pl.ds(off, tt)] = jnp.sum(jnp.where(hots[k], rank, 0.0), axis=0, keepdims=True)
        return cnt + jnp.sum(sel, axis=1, keepdims=True)

    cnt = lax.fori_loop(0, nt, tile, jnp.zeros((N_EXPERTS, 1), F32))
    cnt_ref[...] = cnt.astype(I32)
    padded = jnp.ceil(cnt * (1.0 / row_align)) * row_align
    er = lax.broadcasted_iota(I32, (N_EXPERTS, N_EXPERTS), 0)
    ec = lax.broadcasted_iota(I32, (N_EXPERTS, N_EXPERTS), 1)
    lower = jnp.where(ec < er, 1.0, 0.0)
    offs = jnp.dot(lower, jnp.broadcast_to(padded, (N_EXPERTS, LANES)), preferred_element_type=F32,
                   precision=lax.Precision.HIGHEST)[:, 0:1]
    off_ref[...] = offs.astype(I32)

    def place(ti, carry):
        off = pl.multiple_of(ti * tt, tt)
        for k in range(TOP_K):
            hot = eio == idx_ref[k:k + 1, pl.ds(off, tt)]
            base = jnp.sum(jnp.where(hot, offs, 0.0), axis=0, keepdims=True)
            pos_ref[k:k + 1, pl.ds(off, tt)] = (base + rank_ref[k:k + 1, pl.ds(off, tt)]).astype(I32)
        return carry

    lax.fori_loop(0, nt, place, 0)


def _route(logits_t, *, tt, row_align):
    t = logits_t.shape[1]
    full = lambda shape: pl.BlockSpec(shape, lambda i: (0, 0))
    return pl.pallas_call(
        functools.partial(_route_kernel, tt=tt, row_align=row_align),
        grid=(1,),
        in_specs=[full((N_EXPERTS, t))],
        out_specs=[full((TOP_K, t)), full((TOP_K, t)), full((TOP_K, t)),
                   full((N_EXPERTS, 1)), full((N_EXPERTS, 1))],
        out_shape=[
            jax.ShapeDtypeStruct((TOP_K, t), I32),
            jax.ShapeDtypeStruct((TOP_K, t), F32),
            jax.ShapeDtypeStruct((TOP_K, t), I32),
            jax.ShapeDtypeStruct((N_EXPERTS, 1), I32),
            jax.ShapeDtypeStruct((N_EXPERTS, 1), I32),
        ],
        scratch_shapes=[pltpu.VMEM((TOP_K, t), F32)],
        compiler_params=pltpu.CompilerParams(
            dimension_semantics=("arbitrary",), vmem_limit_bytes=VMEM_LIMIT),
        name="route",
    )(logits_t)


def _dispatch_kernel(pos_ref, h_ref, xs_ref, sem, *, tm, t_total):
    i = pl.program_id(0)

    def issue(t, carry):
        for k in range(TOP_K):
            p = pos_ref[k * t_total + i * tm + t]
            pltpu.make_async_copy(h_ref.at[pl.ds(t, 1), :], xs_ref.at[pl.ds(p, 1), :], sem).start()
        return carry

    lax.fori_loop(0, tm, issue, 0)
    for k in range(TOP_K):
        pltpu.make_async_copy(h_ref, xs_ref.at[pl.ds(0, tm), :], sem).wait()


def _dispatch(pos_flat, h2, *, tm, rows_alloc):
    t = h2.shape[0]
    return pl.pallas_call(
        functools.partial(_dispatch_kernel, tm=tm, t_total=t),
        grid_spec=pltpu.PrefetchScalarGridSpec(
            num_scalar_prefetch=1,
            grid=(t // tm,),
            in_specs=[pl.BlockSpec((tm, D_MODEL), lambda i, pos: (i, 0))],
            out_specs=pl.BlockSpec(memory_space=pl.ANY),
            scratch_shapes=[pltpu.SemaphoreType.DMA(())],
        ),
        out_shape=jax.ShapeDtypeStruct((rows_alloc, D_MODEL), F32),
        compiler_params=pltpu.CompilerParams(
            dimension_semantics=("arbitrary",), vmem_limit_bytes=VMEM_LIMIT),
        name="dispatch",
    )(pos_flat, h2)


def _experts_kernel(se_ref, ss_ref, sr_ref, sv_ref, xs_ref, wg_ref, wu_ref, wd_ref, bg_ref, bu_ref,
                    bd_ref, ys_ref, xseg, yacc, stage, wgb, wub, wdb, sem_in, sem_out, *, tm, nj):
    s = pl.program_id(0)
    j = pl.program_id(1)
    rows = sr_ref[s]
    start = ss_ref[s]
    nch = (rows + tm - 1) // tm

    def x_copy(c, slot):
        r0 = pl.multiple_of(start + c * tm, SUBLANES)
        return pltpu.make_async_copy(xs_ref.at[pl.ds(r0, tm), :], stage.at[slot], sem_in.at[slot])

    def y_copy(c):
        r0 = pl.multiple_of(start + c * tm, SUBLANES)
        l0 = pl.multiple_of(c * tm, tm)
        return pltpu.make_async_copy(yacc.at[pl.ds(l0, tm), :], ys_ref.at[pl.ds(r0, tm), :], sem_out)

    @pl.when(rows > 0)
    def _work():
        wgb[...] = wg_ref[...].astype(BF16)
        wub[...] = wu_ref[...].astype(BF16)
        wdb[...] = wd_ref[...].astype(BF16)

        @pl.when(j == 0)
        def _load():
            x_copy(0, 0).start()

            def body(c, carry):
                slot = c % 2

                @pl.when(c + 1 < nch)
                def _next():
                    x_copy(c + 1, 1 - slot).start()

                x_copy(c, slot).wait()
                xseg[pl.ds(pl.multiple_of(c * tm, tm), tm), :] = stage[slot].astype(BF16)
                return carry

            lax.fori_loop(0, nch, body, 0)

        def chunk(c, carry):
            l0 = pl.multiple_of(c * tm, tm)
            x = xseg[pl.ds(l0, tm), :]
            g = jnp.dot(x, wgb[...], preferred_element_type=F32) + bg_ref[...]
            u = jnp.dot(x, wub[...], preferred_element_type=F32) + bu_ref[...]
            g = jnp.minimum(g, SWIGLU_LIMIT)
            u = jnp.clip(u, -SWIGLU_LIMIT, SWIGLU_LIMIT)
            act = g * _sigmoid(SWIGLU_ALPHA * g) * (u + 1.0)
            y = jnp.dot(act.astype(BF16), wdb[...], preferred_element_type=F32)

            @pl.when(j == 0)
            def _first():
                yacc[pl.ds(l0, tm), :] = y + bd_ref[...]

            @pl.when(j > 0)
            def _rest():
                yacc[pl.ds(l0, tm), :] += y

            return carry

        lax.fori_loop(0, nch, chunk, 0)

        @pl.when(j == nj - 1)
        def _flush():
            def start_one(c, carry):
                y_copy(c).start()
                return carry

            def wait_one(c, carry):
                y_copy(c).wait()
                return carry

            lax.fori_loop(0, nch, start_one, 0)
            lax.fori_loop(0, nch, wait_one, 0)


def _experts(seg_e, seg_start, seg_rows, seg_valid, xs, w_gate_up, b_gate_up, w_down, b_down,
             *, tm, tf, cap, nseg):
    nj = D_FF // tf
    rows_alloc = xs.shape[0]

    def jsel(j, sv, s):
        return j * sv[s] + (nj - 1) * (1 - sv[s])

    return pl.pallas_call(
        functools.partial(_experts_kernel, tm=tm, nj=nj),
        grid_spec=pltpu.PrefetchScalarGridSpec(
            num_scalar_prefetch=4,
            grid=(nseg, nj),
            in_specs=[
                pl.BlockSpec(memory_space=pl.ANY),
                pl.BlockSpec((None, D_MODEL, tf), lambda s, j, se, ss, sr, sv: (se[s], 0, jsel(j, sv, s))),
                pl.BlockSpec((None, D_MODEL, tf), lambda s, j, se, ss, sr, sv: (se[s], 0, nj + jsel(j, sv, s))),
                pl.BlockSpec((None, tf, D_MODEL), lambda s, j, se, ss, sr, sv: (se[s], jsel(j, sv, s), 0)),
                pl.BlockSpec((None, 1, tf), lambda s, j, se, ss, sr, sv: (se[s], 0, jsel(j, sv, s))),
                pl.BlockSpec((None, 1, tf), lambda s, j, se, ss, sr, sv: (se[s], 0, nj + jsel(j, sv, s))),
                pl.BlockSpec((None, 1, D_MODEL), lambda s, j, se, ss, sr, sv: (se[s], 0, 0)),
            ],
            out_specs=pl.BlockSpec(memory_space=pl.ANY),
            scratch_shapes=[
                pltpu.VMEM((cap, D_MODEL), BF16),
                pltpu.VMEM((cap, D_MODEL), F32),
                pltpu.VMEM((2, tm, D_MODEL), F32),
                pltpu.VMEM((D_MODEL, tf), BF16),
                pltpu.VMEM((D_MODEL, tf), BF16),
                pltpu.VMEM((tf, D_MODEL), BF16),
                pltpu.SemaphoreType.DMA((2,)),
                pltpu.SemaphoreType.DMA(()),
            ],
        ),
        out_shape=jax.ShapeDtypeStruct((rows_alloc, D_MODEL), F32),
        compiler_params=pltpu.CompilerParams(
            dimension_semantics=("arbitrary", "arbitrary"), vmem_limit_bytes=VMEM_LIMIT),
        name="experts",
    )(seg_e, seg_start, seg_rows, seg_valid, xs, w_gate_up, w_gate_up, w_down,
      b_gate_up, b_gate_up, b_down)


def _combine_kernel(pos_ref, x2_ref, gate_ref, nf_ref, ys_ref, o_ref, buf, sem, *, tm, t_total):
    i = pl.program_id(0)

    def issue(t, carry):
        for k in range(TOP_K):
            p = pos_ref[k * t_total + i * tm + t]
            pltpu.make_async_copy(ys_ref.at[pl.ds(p, 1), :], buf.at[k, pl.ds(t, 1), :], sem).start()
        return carry

    lax.fori_loop(0, tm, issue, 0)
    for k in range(TOP_K):
        pltpu.make_async_copy(ys_ref.at[pl.ds(0, tm), :], buf.at[k], sem).wait()

    g = gate_ref[...]
    x = x2_ref[...]
    for k in range(TOP_K):
        x = x + g[:, k:k + 1] * buf[k]
    ms = jnp.mean(x * x, axis=-1, keepdims=True)
    o_ref[...] = x * lax.rsqrt(ms + EPS) * nf_ref[...]


def _combine(pos_flat, x2, gates_t, norm_f_w, ys, *, tm):
    t = x2.shape[0]
    return pl.pallas_call(
        functools.partial(_combine_kernel, tm=tm, t_total=t),
        grid_spec=pltpu.PrefetchScalarGridSpec(
            num_scalar_prefetch=1,
            grid=(t // tm,),
            in_specs=[
                pl.BlockSpec((tm, D_MODEL), lambda i, pos: (i, 0)),
                pl.BlockSpec((tm, TOP_K), lambda i, pos: (i, 0)),
                pl.BlockSpec((1, D_MODEL), lambda i, pos: (0, 0)),
                pl.BlockSpec(memory_space=pl.ANY),
            ],
            out_specs=pl.BlockSpec((tm, D_MODEL), lambda i, pos: (i, 0)),
            scratch_shapes=[pltpu.VMEM((TOP_K, tm, D_MODEL), F32), pltpu.SemaphoreType.DMA(())],
        ),
        out_shape=jax.ShapeDtypeStruct((t, D_MODEL), F32),
        compiler_params=pltpu.CompilerParams(
            dimension_semantics=("arbitrary",), vmem_limit_bytes=VMEM_LIMIT),
        name="combine",
    )(pos_flat, x2, gates_t, norm_f_w, ys)


def _tiles(t):
    return dict(
        in_tm=min(1024, t), in_tn=512,
        attn_tq=256,
        out_tm=min(256, t),
        route_tt=min(512, t),
        disp_tm=min(256, t),
        moe_tm=256, moe_tf=512, moe_cap=min(1280, TOP_K * t),
        comb_tm=min(256, t),
    )


def _segments(cnt, offs, *, cap, nseg):
    nseg_e = (cnt + cap - 1) // cap
    ends = jnp.cumsum(nseg_e)
    total = ends[-1]
    s = jnp.arange(nseg, dtype=I32)
    valid = s < total
    s_eff = jnp.minimum(s, jnp.maximum(total - 1, 0))
    e = jnp.minimum(jnp.searchsorted(ends, s_eff, side="right"), N_EXPERTS - 1).astype(I32)
    local = s_eff - (ends[e] - nseg_e[e])
    start = offs[e] + local * cap
    rows = jnp.where(valid, jnp.clip(cnt[e] - local * cap, 0, cap), 0)
    return e, start.astype(I32), rows.astype(I32), valid.astype(I32)


def kernel(x, positions, norm1_w, w_in, lambda_q1, lambda_k1, lambda_q2, lambda_k2, attn_subln_w,
           conv_w, conv_b, dt_bias, a_log, d_skip, ssd_norm_w, w_out, norm2_w, router_w, router_b,
           w_gate_up, b_gate_up, w_down, b_down, norm_f_w):
    bsz, seq, _ = x.shape
    t = bsz * seq
    depth = w_in.shape[0]
    assert depth == 1, "single-layer block"
    cfg = _tiles(t)
    layer = 0

    x2d = x.reshape(t, D_MODEL)
    pos2d = positions.reshape(t, 1).astype(I32)
    inv_freq = ROPE_THETA ** (-jnp.arange(0, ROT_DIM, 2, dtype=F32) / ROT_DIM)
    d = jnp.arange(LANES) % ATTN_DH
    invf = jnp.where(d < ROT_DIM, inv_freq[d % ROT_HALF], 0.0).reshape(1, LANES).astype(F32)

    proj = _in_proj(x2d, pos2d, invf, norm1_w[layer].reshape(1, D_MODEL), w_in[layer],
                    tm=cfg["in_tm"], tn=cfg["in_tn"])

    vec = lambda a: a[layer].reshape(1, -1).astype(F32)
    attn = _attention(proj, vec(lambda_q1), vec(lambda_k1), vec(lambda_q2), vec(lambda_k2),
                      vec(attn_subln_w), bsz=bsz, seq=seq, tq=min(cfg["attn_tq"], seq))

    pad_heads = lambda a: jnp.pad(a[layer].astype(F32), (0, LANES - SSD_HEADS)).reshape(1, LANES)
    head_of_lane = jnp.arange(SSD_WIDTH) // SSD_HEADDIM
    expand = (jnp.arange(LANES)[:, None] == head_of_lane[None, :]).astype(BF16)
    dskip_x = d_skip[layer].astype(F32)[head_of_lane].reshape(1, SSD_WIDTH)
    ssd = _ssd(proj, conv_w[layer], conv_b[layer].reshape(1, CONV_DIM), pad_heads(dt_bias),
               pad_heads(a_log), dskip_x, ssd_norm_w[layer].reshape(1, SSD_WIDTH), expand,
               bsz=bsz, seq=seq)

    x2, h2, logits_t = _out_proj(
        x2d, attn, ssd, w_out[layer].astype(BF16), norm2_w[layer].reshape(1, D_MODEL),
        router_w[layer].T, router_b[layer].reshape(N_EXPERTS, 1), tm=cfg["out_tm"])

    idx, gates, pos, cnt, offs = _route(logits_t, tt=cfg["route_tt"], row_align=SUBLANES)
    del idx
    pos_flat = pos.reshape(TOP_K * t)

    cap, tm = cfg["moe_cap"], cfg["moe_tm"]
    nseg = N_EXPERTS + (TOP_K * t) // cap
    rows_alloc = TOP_K * t + N_EXPERTS * SUBLANES + tm
    seg_e, seg_start, seg_rows, seg_valid = _segments(
        cnt.reshape(N_EXPERTS), offs.reshape(N_EXPERTS), cap=cap, nseg=nseg)

    xs = _dispatch(pos_flat, h2, tm=cfg["disp_tm"], rows_alloc=rows_alloc)
    ys = _experts(seg_e, seg_start, seg_rows, seg_valid, xs, w_gate_up[layer],
                  b_gate_up[layer].reshape(N_EXPERTS, 1, 2 * D_FF), w_down[layer],
                  b_down[layer].reshape(N_EXPERTS, 1, D_MODEL),
                  tm=tm, tf=cfg["moe_tf"], cap=cap, nseg=nseg)
    out = _combine(pos_flat, x2, gates.T, norm_f_w.reshape(1, D_MODEL), ys, tm=cfg["comb_tm"])
    return out.reshape(bsz, seq, D_MODEL)
```

```python
import functools
import math

import jax
import jax.numpy as jnp
from jax import lax
from jax.experimental import pallas as pl
from jax.experimental.pallas import tpu as pltpu

F32 = jnp.float32
BF16 = jnp.bfloat16
I32 = jnp.int32

D_MODEL = 2048
ATTN_WIDTH = 1024
SSD_WIDTH = 1024
ATTN_HEADS = 8
ATTN_DH = 64
ATTN_DV = 128
ROT_DIM = 16
ROT_HALF = ROT_DIM // 2
ROPE_THETA = 500000.0
SSD_HEADDIM = 64
SSD_HEADS = 16
SSD_GROUPS = 2
SSD_STATE = 128
CONV_W = 4
CONV_DIM = SSD_WIDTH + 2 * SSD_GROUPS * SSD_STATE
CHUNK = 128
N_EXPERTS = 32
TOP_K = 4
D_FF = 2048
SWIGLU_ALPHA = 1.702
SWIGLU_LIMIT = 7.0
EPS = 1e-5
IN_TOTAL = 3 * ATTN_WIDTH + SSD_WIDTH + CONV_DIM + SSD_HEADS
LAMBDA_INIT = 0.8 - 0.6 * math.exp(-0.3 * 0)

LANES = 128
SUBLANES = 8
VMEM_LIMIT = 56 * 1024 * 1024
ISSUE_UNROLL = 4

OFF_Q = 0
OFF_K = ATTN_WIDTH
OFF_V = 2 * ATTN_WIDTH
OFF_Z = 0
OFF_X = OFF_Z + SSD_WIDTH
OFF_B = OFF_X + SSD_WIDTH
OFF_C = OFF_B + SSD_GROUPS * SSD_STATE
OFF_DT = OFF_C + SSD_GROUPS * SSD_STATE


def _sigmoid(x):
    return 1.0 / (1.0 + jnp.exp(-x))


def _softplus(x):
    return jnp.maximum(x, 0.0) + jnp.log1p(jnp.exp(-jnp.abs(x)))


def _in_proj_kernel(pos_ref, invf_ref, x_ref, nw_ref, w_ref, qkv_ref, rest_ref,
                    hn_ref, c_ref, sa_ref, sb_ref, *, tn, n_q_tiles, n_rope_tiles, n_qkv_tiles):
    j = pl.program_id(1)

    @pl.when(j == 0)
    def _prepare():
        x = x_ref[...]
        ms = jnp.mean(x * x, axis=-1, keepdims=True)
        hn_ref[...] = (x * lax.rsqrt(ms + EPS) * nw_ref[...]).astype(BF16)
        ang = pos_ref[...].astype(F32) * invf_ref[...]
        d = lax.broadcasted_iota(I32, ang.shape, 1) % ATTN_DH
        cos = jnp.cos(ang)
        sin = jnp.sin(ang)
        c_ref[...] = jnp.where(d < ROT_DIM, cos, 1.0)
        sa_ref[...] = jnp.where(d < ROT_HALF, 0.0, jnp.where(d < ROT_DIM, sin, 0.0))
        sb_ref[...] = jnp.where(d < ROT_HALF, -sin, 0.0)

    acc = jnp.dot(hn_ref[...], w_ref[...].astype(BF16), preferred_element_type=F32)

    @pl.when(j < n_rope_tiles)
    def _rope():
        reps = tn // LANES
        c = jnp.tile(c_ref[...], (1, reps))
        sa = jnp.tile(sa_ref[...], (1, reps))
        sb = jnp.tile(sb_ref[...], (1, reps))
        r = acc * c + pltpu.roll(acc, ROT_HALF, 1) * sa + pltpu.roll(acc, tn - ROT_HALF, 1) * sb
        scale = jnp.where(j < n_q_tiles, ATTN_DH ** -0.5, 1.0)
        qkv_ref[...] = (r * scale).astype(qkv_ref.dtype)

    @pl.when((j >= n_rope_tiles) & (j < n_qkv_tiles))
    def _value():
        qkv_ref[...] = acc.astype(qkv_ref.dtype)

    @pl.when(j >= n_qkv_tiles)
    def _rest():
        rest_ref[...] = acc


def _in_proj(x2d, pos2d, invf, norm_w, w_in, *, tm, tn):
    t = x2d.shape[0]
    n = w_in.shape[1]
    n_qkv = 3 * ATTN_WIDTH
    nqt = n_qkv // tn
    grid = (t // tm, pl.cdiv(n, tn))
    kern = functools.partial(_in_proj_kernel, tn=tn, n_q_tiles=ATTN_WIDTH // tn,
                             n_rope_tiles=2 * ATTN_WIDTH // tn, n_qkv_tiles=nqt)
    return pl.pallas_call(
        kern,
        grid=grid,
        in_specs=[
            pl.BlockSpec((tm, 1), lambda i, j: (i, 0)),
            pl.BlockSpec((1, LANES), lambda i, j: (0, 0)),
            pl.BlockSpec((tm, D_MODEL), lambda i, j: (i, 0)),
            pl.BlockSpec((1, D_MODEL), lambda i, j: (0, 0)),
            pl.BlockSpec((D_MODEL, tn), lambda i, j: (0, j)),
        ],
        out_specs=[
            pl.BlockSpec((tm, tn), lambda i, j: (i, jnp.minimum(j, nqt - 1))),
            pl.BlockSpec((tm, tn), lambda i, j: (i, jnp.maximum(j - nqt, 0))),
        ],
        out_shape=[jax.ShapeDtypeStruct((t, n_qkv), BF16),
                   jax.ShapeDtypeStruct((t, n - n_qkv), F32)],
        scratch_shapes=[
            pltpu.VMEM((tm, D_MODEL), BF16),
            pltpu.VMEM((tm, LANES), F32),
            pltpu.VMEM((tm, LANES), F32),
            pltpu.VMEM((tm, LANES), F32),
        ],
        compiler_params=pltpu.CompilerParams(
            dimension_semantics=("parallel", "arbitrary"), vmem_limit_bytes=VMEM_LIMIT),
        name="in_proj",
    )(pos2d, invf, x2d, norm_w, w_in)


def _attn_kernel(lq1_ref, lk1_ref, lq2_ref, lk2_ref, q_ref, k_ref, v_ref, w_ref, o_ref, *, tq, hp):
    i = pl.program_id(2)
    lane = lax.broadcasted_iota(I32, (tq, ATTN_DV), 1)
    qs = []
    for h in range(hp):
        q = q_ref[:, h * ATTN_DV:(h + 1) * ATTN_DV]
        zero = jnp.zeros_like(q)
        qs.append(jnp.concatenate([jnp.where(lane < ATTN_DH, q, zero),
                                   jnp.where(lane < ATTN_DH, zero, q)], axis=0))

    def block(n, carry, masked):
        off = pl.multiple_of(n * tq, tq)
        out = []
        for h in range(hp):
            m, l, acc = carry[h]
            kb = k_ref[pl.ds(off, tq), h * ATTN_DV:(h + 1) * ATTN_DV]
            vb = v_ref[pl.ds(off, tq), h * ATTN_DV:(h + 1) * ATTN_DV]
            s = lax.dot_general(qs[h], kb, (((1,), (1,)), ((), ())), preferred_element_type=F32)
            if masked:
                r = lax.broadcasted_iota(I32, s.shape, 0) % tq
                c = lax.broadcasted_iota(I32, s.shape, 1)
                s = jnp.where(c <= r, s, -jnp.inf)
            m_new = jnp.maximum(m, jnp.max(s, axis=-1, keepdims=True))
            a = jnp.exp(m - m_new)
            p = jnp.exp(s - m_new)
            l = a * l + jnp.sum(p, axis=-1, keepdims=True)
            acc = a * acc + jnp.dot(p.astype(BF16), vb, preferred_element_type=F32)
            out.append((m_new, l, acc))
        return tuple(out)

    init = tuple((jnp.full((2 * tq, 1), -jnp.inf, F32), jnp.zeros((2 * tq, 1), F32),
                  jnp.zeros((2 * tq, ATTN_DV), F32)) for _ in range(hp))
    carry = lax.fori_loop(0, i, lambda n, c: block(n, c, False), init)
    carry = block(i, carry, True)

    lam = (jnp.exp(jnp.sum(lq1_ref[...] * lk1_ref[...], axis=-1, keepdims=True))
           - jnp.exp(jnp.sum(lq2_ref[...] * lk2_ref[...], axis=-1, keepdims=True)) + LAMBDA_INIT)
    for h in range(hp):
        _, l, acc = carry[h]
        o = acc / l
        d = o[:tq] - lam * o[tq:]
        ms = jnp.mean(d * d, axis=-1, keepdims=True)
        o_ref[:, h * ATTN_DV:(h + 1) * ATTN_DV] = (
            (d * lax.rsqrt(ms + EPS) * w_ref[...]) * (1.0 - LAMBDA_INIT)).astype(o_ref.dtype)


def _attention(qkv, lq1, lk1, lq2, lk2, subln_w, *, bsz, seq, tq, hp):
    nq = seq // tq
    bw = hp * ATTN_DV
    kb0 = OFF_K // bw
    vb0 = OFF_V // bw
    vec = pl.BlockSpec((1, ATTN_DH), lambda b, h, i: (0, 0))
    return pl.pallas_call(
        functools.partial(_attn_kernel, tq=tq, hp=hp),
        grid=(bsz, ATTN_HEADS // hp, nq),
        in_specs=[
            vec, vec, vec, vec,
            pl.BlockSpec((tq, bw), lambda b, h, i: (b * nq + i, h)),
            pl.BlockSpec((seq, bw), lambda b, h, i: (b, kb0 + h)),
            pl.BlockSpec((seq, bw), lambda b, h, i: (b, vb0 + h)),
            pl.BlockSpec((1, ATTN_DV), lambda b, h, i: (0, 0)),
        ],
        out_specs=pl.BlockSpec((tq, bw), lambda b, h, i: (b * nq + i, h)),
        out_shape=jax.ShapeDtypeStruct((bsz * seq, ATTN_WIDTH), BF16),
        compiler_params=pltpu.CompilerParams(
            dimension_semantics=("parallel", "parallel", "arbitrary"), vmem_limit_bytes=VMEM_LIMIT),
        name="diff_attention",
    )(lq1, lk1, lq2, lk2, qkv, qkv, qkv, subln_w)


def _ssd_kernel(z_ref, xs_ref, b_ref, c_ref, dt_ref, cw_ref, cb_ref, dtb_ref, alog_ref,
                dskip_ref, nw_ref, e_ref, o_ref, ext_ref, state_ref):
    ci = pl.program_id(1)
    q = CHUNK
    gw = SSD_WIDTH // SSD_GROUPS

    @pl.when(ci == 0)
    def _init():
        ext_ref[0:SUBLANES, :] = jnp.zeros((SUBLANES, CONV_DIM), F32)
        state_ref[...] = jnp.zeros_like(state_ref)

    u = jnp.concatenate([xs_ref[...], b_ref[...], c_ref[...]], axis=1)
    ext_ref[SUBLANES:SUBLANES + q, :] = u
    acc = jnp.broadcast_to(cb_ref[...], (q, CONV_DIM))
    for w in range(CONV_W):
        acc = acc + ext_ref[pl.ds(SUBLANES - (CONV_W - 1) + w, q), :] * cw_ref[w:w + 1, :]
    ext_ref[0:SUBLANES, :] = u[q - SUBLANES:q, :]
    xbc = acc * _sigmoid(acc)
    xc = xbc[:, :SSD_WIDTH]
    bm = xbc[:, SSD_WIDTH:SSD_WIDTH + SSD_GROUPS * SSD_STATE]
    cm = xbc[:, SSD_WIDTH + SSD_GROUPS * SSD_STATE:]

    lane = lax.broadcasted_iota(I32, (q, LANES), 1)
    head_ok = lane < SSD_HEADS
    dt = jnp.where(head_ok, _softplus(jnp.where(head_ok, dt_ref[...], 0.0) + dtb_ref[...]), 0.0)
    da = dt * (-jnp.exp(alog_ref[...]))
    row = lax.broadcasted_iota(I32, (q, q), 0)
    col = lax.broadcasted_iota(I32, (q, q), 1)
    causal = row >= col
    cs = jnp.dot(causal.astype(F32), da, preferred_element_type=F32,
                 precision=lax.Precision.HIGHEST)
    cs_t = cs.T
    cs_last = cs[q - 1:q, :]

    stk = jnp.concatenate([dt, jnp.exp(cs), jnp.exp(cs_last - cs)], axis=0)
    hi = stk.astype(BF16)
    lo = (stk - hi.astype(F32)).astype(BF16)
    ex = (jnp.dot(hi, e_ref[...], preferred_element_type=F32)
          + jnp.dot(lo, e_ref[...], preferred_element_type=F32))
    dtx = ex[0:q]
    ecx = ex[q:2 * q]
    dsx = ex[2 * q:3 * q]

    xdt = xc * dtx
    xdt_b = xdt.astype(BF16)
    xw_b = (xdt * dsx).astype(BF16)
    lane_q = lax.broadcasted_iota(I32, (q, LANES), 1)

    y_groups = []
    for g in range(SSD_GROUPS):
        gs = slice(g * gw, (g + 1) * gw)
        bg = bm[:, g * SSD_STATE:(g + 1) * SSD_STATE]
        cg_b = cm[:, g * SSD_STATE:(g + 1) * SSD_STATE].astype(BF16)
        cb = lax.dot_general(cg_b, bg.astype(BF16), (((1,), (1,)), ((), ())),
                             preferred_element_type=F32)
        prev_t = state_ref[g]
        y_off = jnp.dot(cg_b, prev_t.astype(BF16), preferred_element_type=F32) * ecx[:, gs]
        st_t = jnp.dot(bg.T.astype(BF16), xw_b[:, gs], preferred_element_type=F32)
        state_ref[g] = prev_t * ecx[q - 1:q, gs] + st_t

        pairs = []
        for jp in range(gw // LANES):
            rhs = xdt_b[:, g * gw + jp * LANES: g * gw + (jp + 1) * LANES]
            outs = []
            for hh in range(2):
                h = g * (SSD_HEADS // SSD_GROUPS) + jp * 2 + hh
                seg = (jnp.broadcast_to(cs[:, h:h + 1], (q, q))
                       - jnp.broadcast_to(cs_t[h:h + 1, :], (q, q)))
                dec = jnp.exp(jnp.where(causal, seg, -jnp.inf))
                outs.append(jnp.dot((cb * dec).astype(BF16), rhs, preferred_element_type=F32))
            pairs.append(jnp.where(lane_q < SSD_HEADDIM, outs[0], outs[1]))
        y_groups.append(jnp.concatenate(pairs, axis=1) + y_off)

    y = jnp.concatenate(y_groups, axis=1) + dskip_ref[...] * xc
    z = z_ref[...]
    y = y * (z * _sigmoid(z))
    outs = []
    for g in range(SSD_GROUPS):
        yg = y[:, g * gw:(g + 1) * gw]
        outs.append(yg * lax.rsqrt(jnp.mean(yg * yg, axis=-1, keepdims=True) + EPS))
    o_ref[...] = (jnp.concatenate(outs, axis=1) * nw_ref[...]).astype(o_ref.dtype)


def _ssd(proj, conv_w, conv_b, dt_bias_p, a_log_p, dskip_x, ssd_norm_w, expand, *, bsz, seq):
    nc = seq // CHUNK
    sw = SSD_GROUPS * SSD_STATE
    const = lambda shape: pl.BlockSpec(shape, lambda b, c: (0, 0))
    return pl.pallas_call(
        _ssd_kernel,
        grid=(bsz, nc),
        in_specs=[
            pl.BlockSpec((CHUNK, SSD_WIDTH), lambda b, c: (b * nc + c, OFF_Z // SSD_WIDTH)),
            pl.BlockSpec((CHUNK, SSD_WIDTH), lambda b, c: (b * nc + c, OFF_X // SSD_WIDTH)),
            pl.BlockSpec((CHUNK, sw), lambda b, c: (b * nc + c, OFF_B // sw)),
            pl.BlockSpec((CHUNK, sw), lambda b, c: (b * nc + c, OFF_C // sw)),
            pl.BlockSpec((CHUNK, LANES), lambda b, c: (b * nc + c, OFF_DT // LANES)),
            const((CONV_W, CONV_DIM)),
            const((1, CONV_DIM)),
            const((1, LANES)),
            const((1, LANES)),
            const((1, SSD_WIDTH)),
            const((1, SSD_WIDTH)),
            const((LANES, SSD_WIDTH)),
        ],
        out_specs=pl.BlockSpec((CHUNK, SSD_WIDTH), lambda b, c: (b * nc + c, 0)),
        out_shape=jax.ShapeDtypeStruct((bsz * seq, SSD_WIDTH), BF16),
        scratch_shapes=[
            pltpu.VMEM((SUBLANES + CHUNK, CONV_DIM), F32),
            pltpu.VMEM((SSD_GROUPS, SSD_STATE, SSD_WIDTH // SSD_GROUPS), F32),
        ],
        compiler_params=pltpu.CompilerParams(
            dimension_semantics=("parallel", "arbitrary"), vmem_limit_bytes=VMEM_LIMIT),
        name="ssd",
    )(proj, proj, proj, proj, proj, conv_w, conv_b, dt_bias_p, a_log_p, dskip_x, ssd_norm_w, expand)


def _out_proj_kernel(x_ref, a_ref, s_ref, wo_ref, n2_ref, rw_ref, rb_ref, x2_ref, h2_ref, lg_ref,
                     *, sub):
    for r0 in range(0, x_ref.shape[0], sub):
        rs = slice(r0, r0 + sub)
        cat = jnp.concatenate([a_ref[rs, :], s_ref[rs, :]], axis=1)
        x2 = x_ref[rs, :] + jnp.dot(cat, wo_ref[...], preferred_element_type=F32)
        x2_ref[rs, :] = x2
        ms = jnp.mean(x2 * x2, axis=-1, keepdims=True)
        h2 = x2 * lax.rsqrt(ms + EPS) * n2_ref[...]
        h2_ref[rs, :] = h2
        h_hi = h2.astype(BF16)
        h_mid = (h2 - h_hi.astype(F32)).astype(BF16)
        p1 = jnp.dot(h_hi, rw_ref[...], preferred_element_type=F32)
        p2 = jnp.dot(h_mid, rw_ref[:, 0:LANES], preferred_element_type=F32)
        lg = p1[:, 0:LANES] + p1[:, LANES:] + p2
        lg_ref[:, rs] = lg.T[0:N_EXPERTS, :] + rb_ref[...]


def _out_proj(x2d, attn, ssd, w_out_b, norm2_w, router_w2, router_b, *, tm, sub):
    t = x2d.shape[0]
    const = lambda shape: pl.BlockSpec(shape, lambda i: (0, 0))
    return pl.pallas_call(
        functools.partial(_out_proj_kernel, sub=sub),
        grid=(t // tm,),
        in_specs=[
            pl.BlockSpec((tm, D_MODEL), lambda i: (i, 0)),
            pl.BlockSpec((tm, ATTN_WIDTH), lambda i: (i, 0)),
            pl.BlockSpec((tm, SSD_WIDTH), lambda i: (i, 0)),
            const((D_MODEL, D_MODEL)),
            const((1, D_MODEL)),
            const((D_MODEL, 2 * LANES)),
            const((N_EXPERTS, 1)),
        ],
        out_specs=[
            pl.BlockSpec((tm, D_MODEL), lambda i: (i, 0)),
            pl.BlockSpec((tm, D_MODEL), lambda i: (i, 0)),
            pl.BlockSpec((N_EXPERTS, tm), lambda i: (0, i)),
        ],
        out_shape=[
            jax.ShapeDtypeStruct((t, D_MODEL), F32),
            jax.ShapeDtypeStruct((t, D_MODEL), F32),
            jax.ShapeDtypeStruct((N_EXPERTS, t), F32),
        ],
        compiler_params=pltpu.CompilerParams(
            dimension_semantics=("parallel",), vmem_limit_bytes=VMEM_LIMIT),
        name="out_proj",
    )(x2d, attn, ssd, w_out_b, norm2_w, router_w2, router_b)


def _route_kernel(lg_ref, idx_ref, gate_ref, pos_ref, cnt_ref, off_ref, rank_ref, *, tt, row_align):
    t_total = lg_ref.shape[1]
    nt = t_total // tt
    eio = lax.broadcasted_iota(I32, (N_EXPERTS, tt), 0)
    r = lax.broadcasted_iota(I32, (tt, tt), 0)
    c = lax.broadcasted_iota(I32, (tt, tt), 1)
    before = jnp.where(r < c, 1.0, 0.0).astype(BF16)

    def tile(ti, cnt):
        off = pl.multiple_of(ti * tt, tt)
        work = lg_ref[:, pl.ds(off, tt)]
        vals, hots = [], []
        for k in range(TOP_K):
            m = jnp.max(work, axis=0, keepdims=True)
            ik = jnp.min(jnp.where(work == m, eio, N_EXPERTS), axis=0, keepdims=True)
            hot = eio == ik
            work = jnp.where(hot, -jnp.inf, work)
            idx_ref[k:k + 1, pl.ds(off, tt)] = ik
            vals.append(m)
            hots.append(hot)
        es = [jnp.exp(v - vals[0]) for v in vals]
        inv = 1.0 / (es[0] + es[1] + es[2] + es[3])
        sel = jnp.zeros((N_EXPERTS, tt), F32)
        for k in range(TOP_K):
            gate_ref[k:k + 1, pl.ds(off, tt)] = es[k] * inv
            sel = sel + jnp.where(hots[k], 1.0, 0.0)
        rank = jnp.dot(sel.astype(BF16), before, preferred_element_type=F32) + cnt
        for k in range(TOP_K):
            rank_ref[k:k + 1, pl.ds(off, tt)] = jnp.sum(jnp.where(hots[k], rank, 0.0), axis=0, keepdims=True)
        return cnt + jnp.sum(sel, axis=1, keepdims=True)

    cnt = lax.fori_loop(0, nt, tile, jnp.zeros((N_EXPERTS, 1), F32))
    cnt_ref[...] = cnt.astype(I32)
    padded = jnp.ceil(cnt * (1.0 / row_align)) * row_align
    er = lax.broadcasted_iota(I32, (N_EXPERTS, N_EXPERTS), 0)
    ec = lax.broadcasted_iota(I32, (N_EXPERTS, N_EXPERTS), 1)
    lower = jnp.where(ec < er, 1.0, 0.0)
    offs = jnp.dot(lower, jnp.broadcast_to(padded, (N_EXPERTS, LANES)), preferred_element_type=F32,
                   precision=lax.Precision.HIGHEST)[:, 0:1]
    off_ref[...] = offs.astype(I32)

    def place(ti, carry):
        off = pl.multiple_of(ti * tt, tt)
        for k in range(TOP_K):
            hot = eio == idx_ref[k:k + 1, pl.ds(off, tt)]
            base = jnp.sum(jnp.where(hot, offs, 0.0), axis=0, keepdims=True)
            pos_ref[k:k + 1, pl.ds(off, tt)] = (base + rank_ref[k:k + 1, pl.ds(off, tt)]).astype(I32)
        return carry

    lax.fori_loop(0, nt, place, 0)


def _route(logits_t, *, tt, row_align):
    t = logits_t.shape[1]
    full = lambda shape: pl.BlockSpec(shape, lambda i: (0, 0))
    return pl.pallas_call(
        functools.partial(_route_kernel, tt=tt, row_align=row_align),
        grid=(1,),
        in_specs=[full((N_EXPERTS, t))],
        out_specs=[full((TOP_K, t)), full((TOP_K, t)), full((TOP_K, t)),
                   full((N_EXPERTS, 1)), full((N_EXPERTS, 1))],
        out_shape=[
            jax.ShapeDtypeStruct((TOP_K, t), I32),
            jax.ShapeDtypeStruct((TOP_K, t), F32),
            jax.ShapeDtypeStruct((TOP_K, t), I32),
            jax.ShapeDtypeStruct((N_EXPERTS, 1), I32),
            jax.ShapeDtypeStruct((N_EXPERTS, 1), I32),
        ],
        scratch_shapes=[pltpu.VMEM((TOP_K, t), F32)],
        compiler_params=pltpu.CompilerParams(
            dimension_semantics=("arbitrary",), vmem_limit_bytes=VMEM_LIMIT),
        name="route",
    )(logits_t)


def _dispatch_kernel(pos_ref, h_ref, xs_ref, sem, *, tm, t_total):
    i = pl.program_id(0)

    def issue(tb, carry):
        for tu in range(ISSUE_UNROLL):
            t = tb * ISSUE_UNROLL + tu
            for k in range(TOP_K):
                p = pos_ref[k * t_total + i * tm + t]
                pltpu.make_async_copy(h_ref.at[pl.ds(t, 1), :], xs_ref.at[pl.ds(p, 1), :], sem).start()
        return carry

    lax.fori_loop(0, tm // ISSUE_UNROLL, issue, 0)
    for k in range(TOP_K):
        pltpu.make_async_copy(h_ref, xs_ref.at[pl.ds(0, tm), :], sem).wait()


def _dispatch(pos_flat, h2, *, tm, rows_alloc):
    t = h2.shape[0]
    return pl.pallas_call(
        functools.partial(_dispatch_kernel, tm=tm, t_total=t),
        grid_spec=pltpu.PrefetchScalarGridSpec(
            num_scalar_prefetch=1,
            grid=(t // tm,),
            in_specs=[pl.BlockSpec((tm, D_MODEL), lambda i, pos: (i, 0))],
            out_specs=pl.BlockSpec(memory_space=pl.ANY),
            scratch_shapes=[pltpu.SemaphoreType.DMA(())],
        ),
        out_shape=jax.ShapeDtypeStruct((rows_alloc, D_MODEL), F32),
        compiler_params=pltpu.CompilerParams(
            dimension_semantics=("arbitrary",), vmem_limit_bytes=VMEM_LIMIT),
        name="dispatch",
    )(pos_flat, h2)


def _experts_kernel(se_ref, ss_ref, sr_ref, sv_ref, xs_ref, wg_ref, wu_ref, wd_ref, bg_ref, bu_ref,
                    bd_ref, ys_ref, xseg, yacc, sem_in, sem_out, *, unit, nj):
    s = pl.program_id(0)
    j = pl.program_id(1)
    rows = sr_ref[s]
    start = ss_ref[s]
    nunits = (rows + unit - 1) // unit
    nbig = nunits // 2

    def x_copy(u):
        r0 = pl.multiple_of(start + u * unit, SUBLANES)
        l0 = pl.multiple_of(u * unit, unit)
        return pltpu.make_async_copy(xs_ref.at[pl.ds(r0, unit), :], xseg.at[pl.ds(l0, unit), :],
                                     sem_in.at[u])

    def y_copy(u):
        r0 = pl.multiple_of(start + u * unit, SUBLANES)
        l0 = pl.multiple_of(u * unit, unit)
        return pltpu.make_async_copy(yacc.at[pl.ds(l0, unit), :], ys_ref.at[pl.ds(r0, unit), :], sem_out)

    def chunk(u0, nu):
        m = nu * unit
        l0 = pl.multiple_of(u0 * unit, unit)

        @pl.when(j == 0)
        def _arrived():
            for k in range(nu):
                x_copy(u0 + k).wait()

        x = xseg[pl.ds(l0, m), :].astype(BF16)
        g = jnp.dot(x, wg_ref[...].astype(BF16), preferred_element_type=F32) + bg_ref[...]
        u = jnp.dot(x, wu_ref[...].astype(BF16), preferred_element_type=F32) + bu_ref[...]
        g = jnp.minimum(g, SWIGLU_LIMIT)
        u = jnp.clip(u, -SWIGLU_LIMIT, SWIGLU_LIMIT)
        act = g * _sigmoid(SWIGLU_ALPHA * g) * (u + 1.0)
        y = jnp.dot(act.astype(BF16), wd_ref[...].astype(BF16), preferred_element_type=F32)

        @pl.when(j == 0)
        def _first():
            yacc[pl.ds(l0, m), :] = y + bd_ref[...]

        @pl.when(j > 0)
        def _rest():
            yacc[pl.ds(l0, m), :] += y

        @pl.when(j == nj - 1)
        def _flush():
            for k in range(nu):
                y_copy(u0 + k).start()

    @pl.when(rows > 0)
    def _work():
        @pl.when(j == 0)
        def _fetch():
            def start_one(u, carry):
                x_copy(u).start()
                return carry

            lax.fori_loop(0, nunits, start_one, 0)

        def big(c, carry):
            chunk(2 * c, 2)
            return carry

        lax.fori_loop(0, nbig, big, 0)

        @pl.when(nunits % 2 == 1)
        def _tail():
            chunk(2 * nbig, 1)

        @pl.when(j == nj - 1)
        def _drain():
            def wait_one(u, carry):
                y_copy(u).wait()
                return carry

            lax.fori_loop(0, nunits, wait_one, 0)


def _experts(seg_e, seg_start, seg_rows, seg_valid, xs, w_gate_up, b_gate_up, w_down, b_down,
             *, unit, tf, cap, nseg):
    nj = D_FF // tf
    rows_alloc = xs.shape[0]

    def jsel(j, sv, s):
        return j * sv[s] + (nj - 1) * (1 - sv[s])

    return pl.pallas_call(
        functools.partial(_experts_kernel, unit=unit, nj=nj),
        grid_spec=pltpu.PrefetchScalarGridSpec(
            num_scalar_prefetch=4,
            grid=(nseg, nj),
            in_specs=[
                pl.BlockSpec(memory_space=pl.ANY),
                pl.BlockSpec((None, D_MODEL, tf), lambda s, j, se, ss, sr, sv: (se[s], 0, jsel(j, sv, s))),
                pl.BlockSpec((None, D_MODEL, tf), lambda s, j, se, ss, sr, sv: (se[s], 0, nj + jsel(j, sv, s))),
                pl.BlockSpec((None, tf, D_MODEL), lambda s, j, se, ss, sr, sv: (se[s], jsel(j, sv, s), 0)),
                pl.BlockSpec((None, 1, tf), lambda s, j, se, ss, sr, sv: (se[s], 0, jsel(j, sv, s))),
                pl.BlockSpec((None, 1, tf), lambda s, j, se, ss, sr, sv: (se[s], 0, nj + jsel(j, sv, s))),
                pl.BlockSpec((None, 1, D_MODEL), lambda s, j, se, ss, sr, sv: (se[s], 0, 0)),
            ],
            out_specs=pl.BlockSpec(memory_space=pl.ANY),
            scratch_shapes=[
                pltpu.VMEM((cap, D_MODEL), F32),
                pltpu.VMEM((cap, D_MODEL), F32),
                pltpu.SemaphoreType.DMA((cap // unit,)),
                pltpu.SemaphoreType.DMA(()),
            ],
        ),
        out_shape=jax.ShapeDtypeStruct((rows_alloc, D_MODEL), F32),
        compiler_params=pltpu.CompilerParams(
            dimension_semantics=("arbitrary", "arbitrary"), vmem_limit_bytes=VMEM_LIMIT),
        name="experts",
    )(seg_e, seg_start, seg_rows, seg_valid, xs, w_gate_up, w_gate_up, w_down,
      b_gate_up, b_gate_up, b_down)


def _combine_kernel(pos_ref, x2_ref, gate_ref, nf_ref, ys_ref, o_ref, buf, sem, *, tm, t_total):
    i = pl.program_id(0)
    n = pl.num_programs(0)

    def gather(tile, slot):
        def issue(tb, carry):
            for tu in range(ISSUE_UNROLL):
                t = tb * ISSUE_UNROLL + tu
                for k in range(TOP_K):
                    p = pos_ref[k * t_total + tile * tm + t]
                    pltpu.make_async_copy(ys_ref.at[pl.ds(p, 1), :], buf.at[slot, k, pl.ds(t, 1), :],
                                          sem.at[slot]).start()
            return carry

        lax.fori_loop(0, tm // ISSUE_UNROLL, issue, 0)

    @pl.when(i == 0)
    def _first():
        gather(0, 0)

    @pl.when(i + 1 < n)
    def _ahead():
        gather(i + 1, (i + 1) % 2)

    slot = i % 2
    for k in range(TOP_K):
        pltpu.make_async_copy(ys_ref.at[pl.ds(0, tm), :], buf.at[slot, k], sem.at[slot]).wait()

    g = gate_ref[...]
    x = x2_ref[...]
    for k in range(TOP_K):
        x = x + g[:, k:k + 1] * buf[slot, k]
    ms = jnp.mean(x * x, axis=-1, keepdims=True)
    o_ref[...] = x * lax.rsqrt(ms + EPS) * nf_ref[...]


def _combine(pos_flat, x2, gates_t, norm_f_w, ys, *, tm):
    t = x2.shape[0]
    return pl.pallas_call(
        functools.partial(_combine_kernel, tm=tm, t_total=t),
        grid_spec=pltpu.PrefetchScalarGridSpec(
            num_scalar_prefetch=1,
            grid=(t // tm,),
            in_specs=[
                pl.BlockSpec((tm, D_MODEL), lambda i, pos: (i, 0)),
                pl.BlockSpec((tm, TOP_K), lambda i, pos: (i, 0)),
                pl.BlockSpec((1, D_MODEL), lambda i, pos: (0, 0)),
                pl.BlockSpec(memory_space=pl.ANY),
            ],
            out_specs=pl.BlockSpec((tm, D_MODEL), lambda i, pos: (i, 0)),
            scratch_shapes=[pltpu.VMEM((2, TOP_K, tm, D_MODEL), F32), pltpu.SemaphoreType.DMA((2,))],
        ),
        out_shape=jax.ShapeDtypeStruct((t, D_MODEL), F32),
        compiler_params=pltpu.CompilerParams(
            dimension_semantics=("arbitrary",), vmem_limit_bytes=VMEM_LIMIT),
        name="combine",
    )(pos_flat, x2, gates_t, norm_f_w, ys)


def _tiles(t):
    return dict(
        in_tm=min(1024, t), in_tn=512,
        attn_tq=256, attn_hp=2,
        out_tm=min(512, t), out_sub=256,
        route_tt=min(512, t),
        disp_tm=min(256, t),
        moe_unit=256, moe_tf=512, moe_cap=min(1536, TOP_K * t),
        comb_tm=min(256, t),
    )


def _segments(cnt, offs, *, cap, nseg):
    nseg_e = (cnt + cap - 1) // cap
    ends = jnp.cumsum(nseg_e)
    total = ends[-1]
    s = jnp.arange(nseg, dtype=I32)
    valid = s < total
    s_eff = jnp.minimum(s, jnp.maximum(total - 1, 0))
    e = jnp.minimum(jnp.sum((s_eff[:, None] >= ends[None, :]).astype(I32), axis=1), N_EXPERTS - 1)
    local = s_eff - (ends[e] - nseg_e[e])
    start = offs[e] + local * cap
    rows = jnp.where(valid, jnp.clip(cnt[e] - local * cap, 0, cap), 0)
    return e, start.astype(I32), rows.astype(I32), valid.astype(I32)


def kernel(x, positions, norm1_w, w_in, lambda_q1, lambda_k1, lambda_q2, lambda_k2, attn_subln_w,
           conv_w, conv_b, dt_bias, a_log, d_skip, ssd_norm_w, w_out, norm2_w, router_w, router_b,
           w_gate_up, b_gate_up, w_down, b_down, norm_f_w):
    bsz, seq, _ = x.shape
    t = bsz * seq
    depth = w_in.shape[0]
    assert depth == 1, "single-layer block"
    cfg = _tiles(t)
    layer = 0

    x2d = x.reshape(t, D_MODEL)
    pos2d = positions.reshape(t, 1).astype(I32)
    inv_freq = ROPE_THETA ** (-jnp.arange(0, ROT_DIM, 2, dtype=F32) / ROT_DIM)
    d = jnp.arange(LANES) % ATTN_DH
    invf = jnp.where(d < ROT_DIM, inv_freq[d % ROT_HALF], 0.0).reshape(1, LANES).astype(F32)

    qkv, proj = _in_proj(x2d, pos2d, invf, norm1_w[layer].reshape(1, D_MODEL), w_in[layer],
                         tm=cfg["in_tm"], tn=cfg["in_tn"])

    vec = lambda a: a[layer].reshape(1, -1).astype(F32)
    attn = _attention(qkv, vec(lambda_q1), vec(lambda_k1), vec(lambda_q2), vec(lambda_k2),
                      vec(attn_subln_w), bsz=bsz, seq=seq, tq=min(cfg["attn_tq"], seq),
                      hp=cfg["attn_hp"])

    pad_heads = lambda a: jnp.pad(a[layer].astype(F32), (0, LANES - SSD_HEADS)).reshape(1, LANES)
    head_of_lane = jnp.arange(SSD_WIDTH) // SSD_HEADDIM
    expand = (jnp.arange(LANES)[:, None] == head_of_lane[None, :]).astype(BF16)
    dskip_x = d_skip[layer].astype(F32)[head_of_lane].reshape(1, SSD_WIDTH)
    ssd = _ssd(proj, conv_w[layer], conv_b[layer].reshape(1, CONV_DIM), pad_heads(dt_bias),
               pad_heads(a_log), dskip_x, ssd_norm_w[layer].reshape(1, SSD_WIDTH), expand,
               bsz=bsz, seq=seq)

    rw = router_w[layer].astype(F32)
    rw_hi = rw.astype(BF16)
    rw_mid = (rw - rw_hi.astype(F32)).astype(BF16)
    lane_pad = lambda a: jnp.pad(a, ((0, 0), (0, LANES - N_EXPERTS)))
    router_w2 = jnp.concatenate([lane_pad(rw_hi), lane_pad(rw_mid)], axis=1)
    x2, h2, logits_t = _out_proj(
        x2d, attn, ssd, w_out[layer].astype(BF16), norm2_w[layer].reshape(1, D_MODEL),
        router_w2, router_b[layer].reshape(N_EXPERTS, 1),
        tm=cfg["out_tm"], sub=min(cfg["out_sub"], cfg["out_tm"]))

    idx, gates, pos, cnt, offs = _route(logits_t, tt=cfg["route_tt"], row_align=SUBLANES)
    del idx
    pos_flat = pos.reshape(TOP_K * t)

    cap, unit = cfg["moe_cap"], cfg["moe_unit"]
    nseg = N_EXPERTS + (TOP_K * t) // cap
    rows_alloc = TOP_K * t + N_EXPERTS * SUBLANES + unit
    seg_e, seg_start, seg_rows, seg_valid = _segments(
        cnt.reshape(N_EXPERTS), offs.reshape(N_EXPERTS), cap=cap, nseg=nseg)

    xs = _dispatch(pos_flat, h2, tm=cfg["disp_tm"], rows_alloc=rows_alloc)
    ys = _experts(seg_e, seg_start, seg_rows, seg_valid, xs, w_gate_up[layer],
                  b_gate_up[layer].reshape(N_EXPERTS, 1, 2 * D_FF), w_down[layer],
                  b_down[layer].reshape(N_EXPERTS, 1, D_MODEL),
                  unit=unit, tf=cfg["moe_tf"], cap=cap, nseg=nseg)
    out = _combine(pos_flat, x2, gates.T, norm_f_w.reshape(1, D_MODEL), ys, tm=cfg["comb_tm"])
    return out.reshape(bsz, seq, D_MODEL)
```

```python
import functools
import math

import jax
import jax.numpy as jnp
from jax import lax
from jax.experimental import pallas as pl
from jax.experimental.pallas import tpu as pltpu

F32 = jnp.float32
BF16 = jnp.bfloat16
I32 = jnp.int32

D_MODEL = 2048
ATTN_WIDTH = 1024
SSD_WIDTH = 1024
ATTN_HEADS = 8
ATTN_DH = 64
ATTN_DV = 128
ROT_DIM = 16
ROT_HALF = ROT_DIM // 2
ROPE_THETA = 500000.0
SSD_HEADDIM = 64
SSD_HEADS = 16
SSD_GROUPS = 2
SSD_STATE = 128
CONV_W = 4
CONV_DIM = SSD_WIDTH + 2 * SSD_GROUPS * SSD_STATE
CHUNK = 128
N_EXPERTS = 32
TOP_K = 4
D_FF = 2048
SWIGLU_ALPHA = 1.702
SWIGLU_LIMIT = 7.0
EPS = 1e-5
IN_TOTAL = 3 * ATTN_WIDTH + SSD_WIDTH + CONV_DIM + SSD_HEADS
LAMBDA_INIT = 0.8 - 0.6 * math.exp(-0.3 * 0)

LANES = 128
SUBLANES = 8
VMEM_LIMIT = 56 * 1024 * 1024
ISSUE_UNROLL = 4

OFF_Q = 0
OFF_K = ATTN_WIDTH
OFF_V = 2 * ATTN_WIDTH
OFF_Z = 0
OFF_X = OFF_Z + SSD_WIDTH
OFF_B = OFF_X + SSD_WIDTH
OFF_C = OFF_B + SSD_GROUPS * SSD_STATE
OFF_DT = OFF_C + SSD_GROUPS * SSD_STATE


def _sigmoid(x):
    return 1.0 / (1.0 + jnp.exp(-x))


def _softplus(x):
    return jnp.maximum(x, 0.0) + jnp.log1p(jnp.exp(-jnp.abs(x)))


def _in_proj_kernel(pos_ref, invf_ref, x_ref, nw_ref, w_ref, qkv_ref, rest_ref,
                    hn_ref, c_ref, sa_ref, sb_ref, *, tn, n_q_tiles, n_rope_tiles, n_qkv_tiles):
    j = pl.program_id(1)

    @pl.when(j == 0)
    def _prepare():
        x = x_ref[...]
        ms = jnp.mean(x * x, axis=-1, keepdims=True)
        hn_ref[...] = (x * lax.rsqrt(ms + EPS) * nw_ref[...]).astype(BF16)
        ang = pos_ref[...].astype(F32) * invf_ref[...]
        d = lax.broadcasted_iota(I32, ang.shape, 1) % ATTN_DH
        cos = jnp.cos(ang)
        sin = jnp.sin(ang)
        c_ref[...] = jnp.where(d < ROT_DIM, cos, 1.0)
        sa_ref[...] = jnp.where(d < ROT_HALF, 0.0, jnp.where(d < ROT_DIM, sin, 0.0))
        sb_ref[...] = jnp.where(d < ROT_HALF, -sin, 0.0)

    acc = lax.dot_general(hn_ref[...], w_ref[...].astype(BF16), (((1,), (1,)), ((), ())),
                          preferred_element_type=F32)

    @pl.when(j < n_rope_tiles)
    def _rope():
        reps = tn // LANES
        c = jnp.tile(c_ref[...], (1, reps))
        sa = jnp.tile(sa_ref[...], (1, reps))
        sb = jnp.tile(sb_ref[...], (1, reps))
        r = acc * c + pltpu.roll(acc, ROT_HALF, 1) * sa + pltpu.roll(acc, tn - ROT_HALF, 1) * sb
        scale = jnp.where(j < n_q_tiles, ATTN_DH ** -0.5, 1.0)
        qkv_ref[...] = (r * scale).astype(qkv_ref.dtype)

    @pl.when((j >= n_rope_tiles) & (j < n_qkv_tiles))
    def _value():
        qkv_ref[...] = acc.astype(qkv_ref.dtype)

    @pl.when(j >= n_qkv_tiles)
    def _rest():
        rest_ref[...] = acc


def _in_proj(x2d, pos2d, invf, norm_w, w_in_t, *, tm, tn):
    t = x2d.shape[0]
    n = w_in_t.shape[0]
    n_qkv = 3 * ATTN_WIDTH
    nqt = n_qkv // tn
    grid = (t // tm, pl.cdiv(n, tn))
    kern = functools.partial(_in_proj_kernel, tn=tn, n_q_tiles=ATTN_WIDTH // tn,
                             n_rope_tiles=2 * ATTN_WIDTH // tn, n_qkv_tiles=nqt)
    return pl.pallas_call(
        kern,
        grid=grid,
        in_specs=[
            pl.BlockSpec((tm, 1), lambda i, j: (i, 0)),
            pl.BlockSpec((1, LANES), lambda i, j: (0, 0)),
            pl.BlockSpec((tm, D_MODEL), lambda i, j: (i, 0)),
            pl.BlockSpec((1, D_MODEL), lambda i, j: (0, 0)),
            pl.BlockSpec((tn, D_MODEL), lambda i, j: (j, 0)),
        ],
        out_specs=[
            pl.BlockSpec((tm, tn), lambda i, j: (i, jnp.minimum(j, nqt - 1))),
            pl.BlockSpec((tm, tn), lambda i, j: (i, jnp.maximum(j - nqt, 0))),
        ],
        out_shape=[jax.ShapeDtypeStruct((t, n_qkv), BF16),
                   jax.ShapeDtypeStruct((t, n - n_qkv), F32)],
        scratch_shapes=[
            pltpu.VMEM((tm, D_MODEL), BF16),
            pltpu.VMEM((tm, LANES), F32),
            pltpu.VMEM((tm, LANES), F32),
            pltpu.VMEM((tm, LANES), F32),
        ],
        compiler_params=pltpu.CompilerParams(
            dimension_semantics=("parallel", "arbitrary"), vmem_limit_bytes=VMEM_LIMIT),
        name="in_proj",
    )(pos2d, invf, x2d, norm_w, w_in_t)


def _attn_kernel(lq1_ref, lk1_ref, lq2_ref, lk2_ref, q_ref, k_ref, v_ref, w_ref, o_ref, vt_ref,
                 *, tq, tk, gw, hp):
    i = pl.program_id(2)

    @pl.when(i == 0)
    def _transpose_values():
        vt_ref[...] = v_ref[...].astype(F32).T.astype(BF16)

    sub = lax.broadcasted_iota(I32, (ATTN_DV, tq), 0)
    qts = []
    for h in range(hp):
        qt = q_ref[:, h * ATTN_DV:(h + 1) * ATTN_DV].astype(F32).T
        qts.append(jnp.concatenate([jnp.where(sub < ATTN_DH, qt, 0.0),
                                    jnp.where(sub < ATTN_DH, 0.0, qt)], axis=1).astype(BF16))

    ngroups = 2 * tq // gw
    gaps = [lax.broadcasted_iota(I32, (tk, gw), 0)
            - (lax.broadcasted_iota(I32, (tk, gw), 1) + g * gw) % tq for g in range(ngroups)]
    nfull = i * (tq // tk)
    nblocks = (i + 1) * (tq // tk)

    def block(n, stats, masked):
        off = pl.multiple_of(n * tk, tk)
        limit = i * tq - n * tk
        out = []
        for h in range(hp):
            kb = k_ref[pl.ds(off, tk), h * ATTN_DV:(h + 1) * ATTN_DV]
            vtb = vt_ref[h * ATTN_DV:(h + 1) * ATTN_DV, pl.ds(off, tk)]
            ss = [jnp.dot(kb, qts[h][:, g * gw:(g + 1) * gw], preferred_element_type=F32)
                  for g in range(ngroups)]
            mid = []
            for g in range(ngroups):
                m, l, _ = stats[h][g]
                s = ss[g]
                if masked:
                    s = jnp.where(gaps[g] <= limit, s, -jnp.inf)
                m_new = jnp.maximum(m, jnp.max(s, axis=0, keepdims=True))
                a = jnp.exp(m - m_new)
                p = jnp.exp(s - m_new)
                mid.append((m_new, a * l + jnp.sum(p, axis=0, keepdims=True), a, p.astype(BF16)))
            groups = []
            for g in range(ngroups):
                m_new, l, a, p = mid[g]
                acc = a * stats[h][g][2] + jnp.dot(vtb, p, preferred_element_type=F32)
                groups.append((m_new, l, acc))
            out.append(tuple(groups))
        return tuple(out)

    init = tuple(tuple((jnp.full((1, gw), -jnp.inf, F32), jnp.zeros((1, gw), F32),
                        jnp.zeros((ATTN_DV, gw), F32)) for _ in range(ngroups)) for _ in range(hp))
    stats = lax.fori_loop(0, nfull, lambda n, c: block(n, c, False), init)
    stats = lax.fori_loop(nfull, nblocks, lambda n, c: block(n, c, True), stats)

    lam = (jnp.exp(jnp.sum(lq1_ref[...] * lk1_ref[...], axis=-1, keepdims=True))
           - jnp.exp(jnp.sum(lq2_ref[...] * lk2_ref[...], axis=-1, keepdims=True)) + LAMBDA_INIT)
    for h in range(hp):
        l = jnp.concatenate([stats[h][g][1] for g in range(ngroups)], axis=1)
        acc = jnp.concatenate([stats[h][g][2] for g in range(ngroups)], axis=1)
        o = acc / l
        d = o[:, :tq] - lam * o[:, tq:]
        ms = jnp.mean(d * d, axis=0, keepdims=True)
        y = (d * lax.rsqrt(ms + EPS) * w_ref[...]) * (1.0 - LAMBDA_INIT)
        o_ref[:, h * ATTN_DV:(h + 1) * ATTN_DV] = y.T.astype(o_ref.dtype)


def _attention(qkv, lq1, lk1, lq2, lk2, subln_w, *, bsz, seq, tq, tk, gw, hp):
    nq = seq // tq
    bw = hp * ATTN_DV
    kb0 = OFF_K // bw
    vb0 = OFF_V // bw
    vec = pl.BlockSpec((1, ATTN_DH), lambda b, h, i: (0, 0))
    return pl.pallas_call(
        functools.partial(_attn_kernel, tq=tq, tk=tk, gw=gw, hp=hp),
        grid=(bsz, ATTN_HEADS // hp, nq),
        in_specs=[
            vec, vec, vec, vec,
            pl.BlockSpec((tq, bw), lambda b, h, i: (b * nq + i, h)),
            pl.BlockSpec((seq, bw), lambda b, h, i: (b, kb0 + h)),
            pl.BlockSpec((seq, bw), lambda b, h, i: (b, vb0 + h)),
            pl.BlockSpec((ATTN_DV, 1), lambda b, h, i: (0, 0)),
        ],
        out_specs=pl.BlockSpec((tq, bw), lambda b, h, i: (b * nq + i, h)),
        out_shape=jax.ShapeDtypeStruct((bsz * seq, ATTN_WIDTH), BF16),
        scratch_shapes=[pltpu.VMEM((bw, seq), BF16)],
        compiler_params=pltpu.CompilerParams(
            dimension_semantics=("parallel", "parallel", "arbitrary"), vmem_limit_bytes=VMEM_LIMIT),
        name="diff_attention",
    )(lq1, lk1, lq2, lk2, qkv, qkv, qkv, subln_w)


def _ssd_kernel(z_ref, xs_ref, b_ref, c_ref, dt_ref, cw_ref, cb_ref, dtb_ref, alog_ref,
                dskip_ref, nw_ref, e_ref, o_ref, ext_ref, state_ref):
    ci = pl.program_id(1)
    q = CHUNK
    gw = SSD_WIDTH // SSD_GROUPS

    @pl.when(ci == 0)
    def _init():
        ext_ref[0:SUBLANES, :] = jnp.zeros((SUBLANES, CONV_DIM), F32)
        state_ref[...] = jnp.zeros_like(state_ref)

    u = jnp.concatenate([xs_ref[...], b_ref[...], c_ref[...]], axis=1)
    ext_ref[SUBLANES:SUBLANES + q, :] = u
    acc = jnp.broadcast_to(cb_ref[...], (q, CONV_DIM))
    for w in range(CONV_W):
        acc = acc + ext_ref[pl.ds(SUBLANES - (CONV_W - 1) + w, q), :] * cw_ref[w:w + 1, :]
    ext_ref[0:SUBLANES, :] = u[q - SUBLANES:q, :]
    xbc = acc * _sigmoid(acc)
    xc = xbc[:, :SSD_WIDTH]
    bm = xbc[:, SSD_WIDTH:SSD_WIDTH + SSD_GROUPS * SSD_STATE]
    cm = xbc[:, SSD_WIDTH + SSD_GROUPS * SSD_STATE:]

    lane = lax.broadcasted_iota(I32, (q, LANES), 1)
    head_ok = lane < SSD_HEADS
    dt = jnp.where(head_ok, _softplus(jnp.where(head_ok, dt_ref[...], 0.0) + dtb_ref[...]), 0.0)
    da = dt * (-jnp.exp(alog_ref[...]))
    row = lax.broadcasted_iota(I32, (q, q), 0)
    col = lax.broadcasted_iota(I32, (q, q), 1)
    causal = row >= col
    cs = jnp.dot(causal.astype(F32), da, preferred_element_type=F32,
                 precision=lax.Precision.HIGHEST)
    cs_t = cs.T
    cs_last = cs[q - 1:q, :]

    stk = jnp.concatenate([dt, jnp.exp(cs), jnp.exp(cs_last - cs)], axis=0)
    hi = stk.astype(BF16)
    lo = (stk - hi.astype(F32)).astype(BF16)
    ex = (jnp.dot(hi, e_ref[...], preferred_element_type=F32)
          + jnp.dot(lo, e_ref[...], preferred_element_type=F32))
    dtx = ex[0:q]
    ecx = ex[q:2 * q]
    dsx = ex[2 * q:3 * q]

    xdt = xc * dtx
    xdt_b = xdt.astype(BF16)
    xw_b = (xdt * dsx).astype(BF16)
    lane_q = lax.broadcasted_iota(I32, (q, LANES), 1)

    y_groups = []
    for g in range(SSD_GROUPS):
        gs = slice(g * gw, (g + 1) * gw)
        bg = bm[:, g * SSD_STATE:(g + 1) * SSD_STATE]
        cg_b = cm[:, g * SSD_STATE:(g + 1) * SSD_STATE].astype(BF16)
        cb = lax.dot_general(cg_b, bg.astype(BF16), (((1,), (1,)), ((), ())),
                             preferred_element_type=F32)
        prev_t = state_ref[g]
        y_off = jnp.dot(cg_b, prev_t.astype(BF16), preferred_element_type=F32) * ecx[:, gs]
        st_t = jnp.dot(bg.T.astype(BF16), xw_b[:, gs], preferred_element_type=F32)
        state_ref[g] = prev_t * ecx[q - 1:q, gs] + st_t

        pairs = []
        for jp in range(gw // LANES):
            rhs = xdt_b[:, g * gw + jp * LANES: g * gw + (jp + 1) * LANES]
            outs = []
            for hh in range(2):
                h = g * (SSD_HEADS // SSD_GROUPS) + jp * 2 + hh
                seg = (jnp.broadcast_to(cs[:, h:h + 1], (q, q))
                       - jnp.broadcast_to(cs_t[h:h + 1, :], (q, q)))
                dec = jnp.exp(jnp.where(causal, seg, -jnp.inf))
                outs.append(jnp.dot((cb * dec).astype(BF16), rhs, preferred_element_type=F32))
            pairs.append(jnp.where(lane_q < SSD_HEADDIM, outs[0], outs[1]))
        y_groups.append(jnp.concatenate(pairs, axis=1) + y_off)

    y = jnp.concatenate(y_groups, axis=1) + dskip_ref[...] * xc
    z = z_ref[...]
    y = y * (z * _sigmoid(z))
    outs = []
    for g in range(SSD_GROUPS):
        yg = y[:, g * gw:(g + 1) * gw]
        outs.append(yg * lax.rsqrt(jnp.mean(yg * yg, axis=-1, keepdims=True) + EPS))
    o_ref[...] = (jnp.concatenate(outs, axis=1) * nw_ref[...]).astype(o_ref.dtype)


def _ssd(proj, conv_w, conv_b, dt_bias_p, a_log_p, dskip_x, ssd_norm_w, expand, *, bsz, seq):
    nc = seq // CHUNK
    sw = SSD_GROUPS * SSD_STATE
    const = lambda shape: pl.BlockSpec(shape, lambda b, c: (0, 0))
    return pl.pallas_call(
        _ssd_kernel,
        grid=(bsz, nc),
        in_specs=[
            pl.BlockSpec((CHUNK, SSD_WIDTH), lambda b, c: (b * nc + c, OFF_Z // SSD_WIDTH)),
            pl.BlockSpec((CHUNK, SSD_WIDTH), lambda b, c: (b * nc + c, OFF_X // SSD_WIDTH)),
            pl.BlockSpec((CHUNK, sw), lambda b, c: (b * nc + c, OFF_B // sw)),
            pl.BlockSpec((CHUNK, sw), lambda b, c: (b * nc + c, OFF_C // sw)),
            pl.BlockSpec((CHUNK, LANES), lambda b, c: (b * nc + c, OFF_DT // LANES)),
            const((CONV_W, CONV_DIM)),
            const((1, CONV_DIM)),
            const((1, LANES)),
            const((1, LANES)),
            const((1, SSD_WIDTH)),
            const((1, SSD_WIDTH)),
            const((LANES, SSD_WIDTH)),
        ],
        out_specs=pl.BlockSpec((CHUNK, SSD_WIDTH), lambda b, c: (b * nc + c, 0)),
        out_shape=jax.ShapeDtypeStruct((bsz * seq, SSD_WIDTH), BF16),
        scratch_shapes=[
            pltpu.VMEM((SUBLANES + CHUNK, CONV_DIM), F32),
            pltpu.VMEM((SSD_GROUPS, SSD_STATE, SSD_WIDTH // SSD_GROUPS), F32),
        ],
        compiler_params=pltpu.CompilerParams(
            dimension_semantics=("parallel", "arbitrary"), vmem_limit_bytes=VMEM_LIMIT),
        name="ssd",
    )(proj, proj, proj, proj, proj, conv_w, conv_b, dt_bias_p, a_log_p, dskip_x, ssd_norm_w, expand)


def _out_proj_kernel(x_ref, a_ref, s_ref, wo_ref, n2_ref, rw_ref, rb_ref, x2_ref, h2_ref, lg_ref,
                     *, sub):
    for r0 in range(0, x_ref.shape[0], sub):
        rs = slice(r0, r0 + sub)
        cat = jnp.concatenate([a_ref[rs, :], s_ref[rs, :]], axis=1)
        x2 = x_ref[rs, :] + jnp.dot(cat, wo_ref[...], preferred_element_type=F32)
        x2_ref[rs, :] = x2
        ms = jnp.mean(x2 * x2, axis=-1, keepdims=True)
        h2 = x2 * lax.rsqrt(ms + EPS) * n2_ref[...]
        h2_ref[rs, :] = h2
        h_hi = h2.astype(BF16)
        h_mid = (h2 - h_hi.astype(F32)).astype(BF16)
        p1 = jnp.dot(h_hi, rw_ref[...], preferred_element_type=F32)
        p2 = jnp.dot(h_mid, rw_ref[:, 0:LANES], preferred_element_type=F32)
        lg = p1[:, 0:LANES] + p1[:, LANES:] + p2
        lg_ref[:, rs] = lg.T[0:N_EXPERTS, :] + rb_ref[...]


def _out_proj(x2d, attn, ssd, w_out_b, norm2_w, router_w2, router_b, *, tm, sub):
    t = x2d.shape[0]
    const = lambda shape: pl.BlockSpec(shape, lambda i: (0, 0))
    return pl.pallas_call(
        functools.partial(_out_proj_kernel, sub=sub),
        grid=(t // tm,),
        in_specs=[
            pl.BlockSpec((tm, D_MODEL), lambda i: (i, 0)),
            pl.BlockSpec((tm, ATTN_WIDTH), lambda i: (i, 0)),
            pl.BlockSpec((tm, SSD_WIDTH), lambda i: (i, 0)),
            const((D_MODEL, D_MODEL)),
            const((1, D_MODEL)),
            const((D_MODEL, 2 * LANES)),
            const((N_EXPERTS, 1)),
        ],
        out_specs=[
            pl.BlockSpec((tm, D_MODEL), lambda i: (i, 0)),
            pl.BlockSpec((tm, D_MODEL), lambda i: (i, 0)),
            pl.BlockSpec((N_EXPERTS, tm), lambda i: (0, i)),
        ],
        out_shape=[
            jax.ShapeDtypeStruct((t, D_MODEL), F32),
            jax.ShapeDtypeStruct((t, D_MODEL), F32),
            jax.ShapeDtypeStruct((N_EXPERTS, t), F32),
        ],
        compiler_params=pltpu.CompilerParams(
            dimension_semantics=("parallel",), vmem_limit_bytes=VMEM_LIMIT),
        name="out_proj",
    )(x2d, attn, ssd, w_out_b, norm2_w, router_w2, router_b)


def _route_kernel(lg_ref, idx_ref, gate_ref, pos_ref, cnt_ref, off_ref, rank_ref, *, tt, row_align):
    t_total = lg_ref.shape[1]
    nt = t_total // tt
    eio = lax.broadcasted_iota(I32, (N_EXPERTS, tt), 0)
    r = lax.broadcasted_iota(I32, (tt, tt), 0)
    c = lax.broadcasted_iota(I32, (tt, tt), 1)
    before = jnp.where(r < c, 1.0, 0.0).astype(BF16)

    def tile(ti, cnt):
        off = pl.multiple_of(ti * tt, tt)
        work = lg_ref[:, pl.ds(off, tt)]
        vals, hots = [], []
        for k in range(TOP_K):
            m = jnp.max(work, axis=0, keepdims=True)
            ik = jnp.min(jnp.where(work == m, eio, N_EXPERTS), axis=0, keepdims=True)
            hot = eio == ik
            work = jnp.where(hot, -jnp.inf, work)
            idx_ref[k:k + 1, pl.ds(off, tt)] = ik
            vals.append(m)
            hots.append(hot)
        es = [jnp.exp(v - vals[0]) for v in vals]
        inv = 1.0 / (es[0] + es[1] + es[2] + es[3])
        sel = jnp.zeros((N_EXPERTS, tt), F32)
        for k in range(TOP_K):
            gate_ref[k:k + 1, pl.ds(off, tt)] = es[k] * inv
            sel = sel + jnp.where(hots[k], 1.0, 0.0)
        rank = jnp.dot(sel.astype(BF16), before, preferred_element_type=F32) + cnt
        for k in range(TOP_K):
            rank_ref[k:k + 1, pl.ds(off, tt)] = jnp.sum(jnp.where(hots[k], rank, 0.0), axis=0, keepdims=True)
        return cnt + jnp.sum(sel, axis=1, keepdims=True)

    cnt = lax.fori_loop(0, nt, tile, jnp.zeros((N_EXPERTS, 1), F32))
    cnt_ref[...] = cnt.astype(I32)
    padded = jnp.ceil(cnt * (1.0 / row_align)) * row_align
    er = lax.broadcasted_iota(I32, (N_EXPERTS, N_EXPERTS), 0)
    ec = lax.broadcasted_iota(I32, (N_EXPERTS, N_EXPERTS), 1)
    lower = jnp.where(ec < er, 1.0, 0.0)
    offs = jnp.dot(lower, jnp.broadcast_to(padded, (N_EXPERTS, LANES)), preferred_element_type=F32,
                   precision=lax.Precision.HIGHEST)[:, 0:1]
    off_ref[...] = offs.astype(I32)

    def place(ti, carry):
        off = pl.multiple_of(ti * tt, tt)
        for k in range(TOP_K):
            hot = eio == idx_ref[k:k + 1, pl.ds(off, tt)]
            base = jnp.sum(jnp.where(hot, offs, 0.0), axis=0, keepdims=True)
            pos_ref[k:k + 1, pl.ds(off, tt)] = (base + rank_ref[k:k + 1, pl.ds(off, tt)]).astype(I32)
        return carry

    lax.fori_loop(0, nt, place, 0)


def _route(logits_t, *, tt, row_align):
    t = logits_t.shape[1]
    full = lambda shape: pl.BlockSpec(shape, lambda i: (0, 0))
    return pl.pallas_call(
        functools.partial(_route_kernel, tt=tt, row_align=row_align),
        grid=(1,),
        in_specs=[full((N_EXPERTS, t))],
        out_specs=[full((TOP_K, t)), full((TOP_K, t)), full((TOP_K, t)),
                   full((N_EXPERTS, 1)), full((N_EXPERTS, 1))],
        out_shape=[
            jax.ShapeDtypeStruct((TOP_K, t), I32),
            jax.ShapeDtypeStruct((TOP_K, t), F32),
            jax.ShapeDtypeStruct((TOP_K, t), I32),
            jax.ShapeDtypeStruct((N_EXPERTS, 1), I32),
            jax.ShapeDtypeStruct((N_EXPERTS, 1), I32),
        ],
        scratch_shapes=[pltpu.VMEM((TOP_K, t), F32)],
        compiler_params=pltpu.CompilerParams(
            dimension_semantics=("arbitrary",), vmem_limit_bytes=VMEM_LIMIT),
        name="route",
    )(logits_t)


def _dispatch_kernel(pos_ref, h_ref, xs_ref, sem, *, tm, t_total):
    i = pl.program_id(0)

    def issue(tb, carry):
        for tu in range(ISSUE_UNROLL):
            t = tb * ISSUE_UNROLL + tu
            for k in range(TOP_K):
                p = pos_ref[k * t_total + i * tm + t]
                pltpu.make_async_copy(h_ref.at[pl.ds(t, 1), :], xs_ref.at[pl.ds(p, 1), :], sem).start()
        return carry

    lax.fori_loop(0, tm // ISSUE_UNROLL, issue, 0)
    for k in range(TOP_K):
        pltpu.make_async_copy(h_ref, xs_ref.at[pl.ds(0, tm), :], sem).wait()


def _dispatch(pos_flat, h2, *, tm, rows_alloc):
    t = h2.shape[0]
    return pl.pallas_call(
        functools.partial(_dispatch_kernel, tm=tm, t_total=t),
        grid_spec=pltpu.PrefetchScalarGridSpec(
            num_scalar_prefetch=1,
            grid=(t // tm,),
            in_specs=[pl.BlockSpec((tm, D_MODEL), lambda i, pos: (i, 0))],
            out_specs=pl.BlockSpec(memory_space=pl.ANY),
            scratch_shapes=[pltpu.SemaphoreType.DMA(())],
        ),
        out_shape=jax.ShapeDtypeStruct((rows_alloc, D_MODEL), F32),
        compiler_params=pltpu.CompilerParams(
            dimension_semantics=("arbitrary",), vmem_limit_bytes=VMEM_LIMIT),
        name="dispatch",
    )(pos_flat, h2)


def _experts_kernel(se_ref, ss_ref, sr_ref, sv_ref, xs_ref, wg_ref, wu_ref, wd_ref, bg_ref, bu_ref,
                    bd_ref, ys_ref, xseg, yacc, sem_in, sem_out, *, unit, nj):
    s = pl.program_id(0)
    j = pl.program_id(1)
    rows = sr_ref[s]
    start = ss_ref[s]
    nunits = (rows + unit - 1) // unit
    sliver = unit // 4
    odd = nunits % 2 == 1
    fuse = odd & (nunits >= 3) & (rows - (nunits - 1) * unit <= sliver)
    nbig = nunits // 2 - fuse.astype(I32)

    def x_copy(u):
        r0 = pl.multiple_of(start + u * unit, SUBLANES)
        l0 = pl.multiple_of(u * unit, unit)
        return pltpu.make_async_copy(xs_ref.at[pl.ds(r0, unit), :], xseg.at[pl.ds(l0, unit), :],
                                     sem_in.at[u])

    def y_copy(u, nrows=unit):
        r0 = pl.multiple_of(start + u * unit, SUBLANES)
        l0 = pl.multiple_of(u * unit, unit)
        return pltpu.make_async_copy(yacc.at[pl.ds(l0, nrows), :], ys_ref.at[pl.ds(r0, nrows), :], sem_out)

    def chunk(u0, m):
        nu = -(-m // unit)
        l0 = pl.multiple_of(u0 * unit, unit)

        @pl.when(j == 0)
        def _arrived():
            for k in range(nu):
                x_copy(u0 + k).wait()

        x = xseg[pl.ds(l0, m), :].astype(BF16)
        g = jnp.dot(x, wg_ref[...].astype(BF16), preferred_element_type=F32) + bg_ref[...]
        u = jnp.dot(x, wu_ref[...].astype(BF16), preferred_element_type=F32) + bu_ref[...]
        g = jnp.minimum(g, SWIGLU_LIMIT)
        u = jnp.clip(u, -SWIGLU_LIMIT, SWIGLU_LIMIT)
        act = (g * _sigmoid(SWIGLU_ALPHA * g) * (u + 1.0)).astype(BF16)

        @pl.when(j == 0)
        def _first():
            yacc[pl.ds(l0, m), :] = bd_ref[...] + jnp.dot(
                act, wd_ref[...].astype(BF16), preferred_element_type=F32)

        @pl.when(j > 0)
        def _rest():
            yacc[pl.ds(l0, m), :] = yacc[pl.ds(l0, m), :] + jnp.dot(
                act, wd_ref[...].astype(BF16), preferred_element_type=F32)

        @pl.when(j == nj - 1)
        def _flush():
            for k in range(m // unit):
                y_copy(u0 + k).start()
            if m % unit:
                y_copy(u0 + m // unit, m % unit).start()

    @pl.when(rows > 0)
    def _work():
        @pl.when(j == 0)
        def _fetch():
            def start_one(u, carry):
                x_copy(u).start()
                return carry

            lax.fori_loop(0, nunits, start_one, 0)

        def big(c, carry):
            chunk(2 * c, 2 * unit)
            return carry

        lax.fori_loop(0, nbig, big, 0)

        @pl.when(fuse)
        def _with_sliver():
            chunk(2 * nbig, 2 * unit + sliver)

        @pl.when(odd & jnp.logical_not(fuse))
        def _tail():
            chunk(2 * nbig, unit)

        @pl.when(j == nj - 1)
        def _drain():
            def wait_one(u, carry):
                y_copy(u).wait()
                return carry

            lax.fori_loop(0, nunits - fuse.astype(I32), wait_one, 0)

            @pl.when(fuse)
            def _sliver():
                y_copy(nunits - 1, sliver).wait()


def _experts(seg_e, seg_start, seg_rows, seg_valid, xs, w_gate_up, b_gate_up, w_down, b_down,
             *, unit, tf, cap, nseg):
    nj = D_FF // tf
    rows_alloc = xs.shape[0]

    def jsel(j, sv, s):
        return j * sv[s] + (nj - 1) * (1 - sv[s])

    return pl.pallas_call(
        functools.partial(_experts_kernel, unit=unit, nj=nj),
        grid_spec=pltpu.PrefetchScalarGridSpec(
            num_scalar_prefetch=4,
            grid=(nseg, nj),
            in_specs=[
                pl.BlockSpec(memory_space=pl.ANY),
                pl.BlockSpec((None, D_MODEL, tf), lambda s, j, se, ss, sr, sv: (se[s], 0, jsel(j, sv, s))),
                pl.BlockSpec((None, D_MODEL, tf), lambda s, j, se, ss, sr, sv: (se[s], 0, nj + jsel(j, sv, s))),
                pl.BlockSpec((None, tf, D_MODEL), lambda s, j, se, ss, sr, sv: (se[s], jsel(j, sv, s), 0)),
                pl.BlockSpec((None, 1, tf), lambda s, j, se, ss, sr, sv: (se[s], 0, jsel(j, sv, s))),
                pl.BlockSpec((None, 1, tf), lambda s, j, se, ss, sr, sv: (se[s], 0, nj + jsel(j, sv, s))),
                pl.BlockSpec((None, 1, D_MODEL), lambda s, j, se, ss, sr, sv: (se[s], 0, 0)),
            ],
            out_specs=pl.BlockSpec(memory_space=pl.ANY),
            scratch_shapes=[
                pltpu.VMEM((cap, D_MODEL), F32),
                pltpu.VMEM((cap, D_MODEL), F32),
                pltpu.SemaphoreType.DMA((cap // unit,)),
                pltpu.SemaphoreType.DMA(()),
            ],
        ),
        out_shape=jax.ShapeDtypeStruct((rows_alloc, D_MODEL), F32),
        compiler_params=pltpu.CompilerParams(
            dimension_semantics=("arbitrary", "arbitrary"), vmem_limit_bytes=VMEM_LIMIT),
        name="experts",
    )(seg_e, seg_start, seg_rows, seg_valid, xs, w_gate_up, w_gate_up, w_down,
      b_gate_up, b_gate_up, b_down)


def _combine_kernel(pos_ref, x2_ref, gate_ref, nf_ref, ys_ref, o_ref, buf, sem, *, tm, t_total):
    i = pl.program_id(0)
    n = pl.num_programs(0)

    def gather(tile, slot):
        def issue(tb, carry):
            for tu in range(ISSUE_UNROLL):
                t = tb * ISSUE_UNROLL + tu
                for k in range(TOP_K):
                    p = pos_ref[k * t_total + tile * tm + t]
                    pltpu.make_async_copy(ys_ref.at[pl.ds(p, 1), :], buf.at[slot, k, pl.ds(t, 1), :],
                                          sem.at[slot]).start()
            return carry

        lax.fori_loop(0, tm // ISSUE_UNROLL, issue, 0)

    @pl.when(i == 0)
    def _first():
        gather(0, 0)

    @pl.when(i + 1 < n)
    def _ahead():
        gather(i + 1, (i + 1) % 2)

    slot = i % 2
    for k in range(TOP_K):
        pltpu.make_async_copy(ys_ref.at[pl.ds(0, tm), :], buf.at[slot, k], sem.at[slot]).wait()

    g = gate_ref[...]
    x = x2_ref[...]
    for k in range(TOP_K):
        x = x + g[:, k:k + 1] * buf[slot, k]
    ms = jnp.mean(x * x, axis=-1, keepdims=True)
    o_ref[...] = x * lax.rsqrt(ms + EPS) * nf_ref[...]


def _combine(pos_flat, x2, gates_t, norm_f_w, ys, *, tm):
    t = x2.shape[0]
    return pl.pallas_call(
        functools.partial(_combine_kernel, tm=tm, t_total=t),
        grid_spec=pltpu.PrefetchScalarGridSpec(
            num_scalar_prefetch=1,
            grid=(t // tm,),
            in_specs=[
                pl.BlockSpec((tm, D_MODEL), lambda i, pos: (i, 0)),
                pl.BlockSpec((tm, TOP_K), lambda i, pos: (i, 0)),
                pl.BlockSpec((1, D_MODEL), lambda i, pos: (0, 0)),
                pl.BlockSpec(memory_space=pl.ANY),
            ],
            out_specs=pl.BlockSpec((tm, D_MODEL), lambda i, pos: (i, 0)),
            scratch_shapes=[pltpu.VMEM((2, TOP_K, tm, D_MODEL), F32), pltpu.SemaphoreType.DMA((2,))],
        ),
        out_shape=jax.ShapeDtypeStruct((t, D_MODEL), F32),
        compiler_params=pltpu.CompilerParams(
            dimension_semantics=("arbitrary",), vmem_limit_bytes=VMEM_LIMIT),
        name="combine",
    )(pos_flat, x2, gates_t, norm_f_w, ys)


def _tiles(t):
    return dict(
        in_tm=min(1024, t), in_tn=512,
        attn_tq=512, attn_tk=256, attn_gw=128, attn_hp=2,
        out_tm=min(512, t), out_sub=256,
        route_tt=min(512, t),
        disp_tm=min(256, t),
        moe_unit=256, moe_tf=512, moe_cap=min(1536, TOP_K * t),
        comb_tm=min(256, t),
    )


def _segments(cnt, offs, *, cap, nseg):
    nseg_e = (cnt + cap - 1) // cap
    ends = jnp.cumsum(nseg_e)
    total = ends[-1]
    s = jnp.arange(nseg, dtype=I32)
    valid = s < total
    s_eff = jnp.minimum(s, jnp.maximum(total - 1, 0))
    e = jnp.minimum(jnp.sum((s_eff[:, None] >= ends[None, :]).astype(I32), axis=1), N_EXPERTS - 1)
    local = s_eff - (ends[e] - nseg_e[e])
    start = offs[e] + local * cap
    rows = jnp.where(valid, jnp.clip(cnt[e] - local * cap, 0, cap), 0)
    return e, start.astype(I32), rows.astype(I32), valid.astype(I32)


def kernel(x, positions, norm1_w, w_in, lambda_q1, lambda_k1, lambda_q2, lambda_k2, attn_subln_w,
           conv_w, conv_b, dt_bias, a_log, d_skip, ssd_norm_w, w_out, norm2_w, router_w, router_b,
           w_gate_up, b_gate_up, w_down, b_down, norm_f_w):
    bsz, seq, _ = x.shape
    t = bsz * seq
    depth = w_in.shape[0]
    assert depth == 1, "single-layer block"
    cfg = _tiles(t)
    layer = 0

    x2d = x.reshape(t, D_MODEL)
    pos2d = positions.reshape(t, 1).astype(I32)
    inv_freq = ROPE_THETA ** (-jnp.arange(0, ROT_DIM, 2, dtype=F32) / ROT_DIM)
    d = jnp.arange(LANES) % ATTN_DH
    invf = jnp.where(d < ROT_DIM, inv_freq[d % ROT_HALF], 0.0).reshape(1, LANES).astype(F32)

    qkv, proj = _in_proj(x2d, pos2d, invf, norm1_w[layer].reshape(1, D_MODEL), w_in[layer].T,
                         tm=cfg["in_tm"], tn=cfg["in_tn"])

    vec = lambda a: a[layer].reshape(1, -1).astype(F32)
    attn = _attention(qkv, vec(lambda_q1), vec(lambda_k1), vec(lambda_q2), vec(lambda_k2),
                      attn_subln_w[layer].astype(F32).reshape(ATTN_DV, 1),
                      bsz=bsz, seq=seq, tq=min(cfg["attn_tq"], seq),
                      tk=min(cfg["attn_tk"], seq), gw=cfg["attn_gw"], hp=cfg["attn_hp"])

    pad_heads = lambda a: jnp.pad(a[layer].astype(F32), (0, LANES - SSD_HEADS)).reshape(1, LANES)
    head_of_lane = jnp.arange(SSD_WIDTH) // SSD_HEADDIM
    expand = (jnp.arange(LANES)[:, None] == head_of_lane[None, :]).astype(BF16)
    dskip_x = d_skip[layer].astype(F32)[head_of_lane].reshape(1, SSD_WIDTH)
    ssd = _ssd(proj, conv_w[layer], conv_b[layer].reshape(1, CONV_DIM), pad_heads(dt_bias),
               pad_heads(a_log), dskip_x, ssd_norm_w[layer].reshape(1, SSD_WIDTH), expand,
               bsz=bsz, seq=seq)

    rw = router_w[layer].astype(F32)
    rw_hi = rw.astype(BF16)
    rw_mid = (rw - rw_hi.astype(F32)).astype(BF16)
    lane_pad = lambda a: jnp.pad(a, ((0, 0), (0, LANES - N_EXPERTS)))
    router_w2 = jnp.concatenate([lane_pad(rw_hi), lane_pad(rw_mid)], axis=1)
    x2, h2, logits_t = _out_proj(
        x2d, attn, ssd, w_out[layer].astype(BF16), norm2_w[layer].reshape(1, D_MODEL),
        router_w2, router_b[layer].reshape(N_EXPERTS, 1),
        tm=cfg["out_tm"], sub=min(cfg["out_sub"], cfg["out_tm"]))

    idx, gates, pos, cnt, offs = _route(logits_t, tt=cfg["route_tt"], row_align=SUBLANES)
    del idx
    pos_flat = pos.reshape(TOP_K * t)

    cap, unit = cfg["moe_cap"], cfg["moe_unit"]
    nseg = N_EXPERTS + (TOP_K * t) // cap
    rows_alloc = TOP_K * t + N_EXPERTS * SUBLANES + unit
    seg_e, seg_start, seg_rows, seg_valid = _segments(
        cnt.reshape(N_EXPERTS), offs.reshape(N_EXPERTS), cap=cap, nseg=nseg)

    xs = _dispatch(pos_flat, h2, tm=cfg["disp_tm"], rows_alloc=rows_alloc)
    ys = _experts(seg_e, seg_start, seg_rows, seg_valid, xs, w_gate_up[layer],
                  b_gate_up[layer].reshape(N_EXPERTS, 1, 2 * D_FF), w_down[layer],
                  b_down[layer].reshape(N_EXPERTS, 1, D_MODEL),
                  unit=unit, tf=cfg["moe_tf"], cap=cap, nseg=nseg)
    out = _combine(pos_flat, x2, gates.T, norm_f_w.reshape(1, D_MODEL), ys, tm=cfg["comb_tm"])
    return out.reshape(bsz, seq, D_MODEL)
```

```python
import functools
import math

import jax
import jax.numpy as jnp
from jax import lax
from jax.experimental import pallas as pl
from jax.experimental.pallas import tpu as pltpu

F32 = jnp.float32
BF16 = jnp.bfloat16
I32 = jnp.int32

D_MODEL = 2048
ATTN_WIDTH = 1024
SSD_WIDTH = 1024
ATTN_HEADS = 8
ATTN_DH = 64
ATTN_DV = 128
ROT_DIM = 16
ROT_HALF = ROT_DIM // 2
ROPE_THETA = 500000.0
SSD_HEADDIM = 64
SSD_HEADS = 16
SSD_GROUPS = 2
SSD_STATE = 128
CONV_W = 4
CONV_DIM = SSD_WIDTH + 2 * SSD_GROUPS * SSD_STATE
CHUNK = 128
N_EXPERTS = 32
TOP_K = 4
D_FF = 2048
SWIGLU_ALPHA = 1.702
SWIGLU_LIMIT = 7.0
EPS = 1e-5
IN_TOTAL = 3 * ATTN_WIDTH + SSD_WIDTH + CONV_DIM + SSD_HEADS
LAMBDA_INIT = 0.8 - 0.6 * math.exp(-0.3 * 0)

LANES = 128
SUBLANES = 8
VMEM_LIMIT = 56 * 1024 * 1024
ISSUE_UNROLL = 4

OFF_Q = 0
OFF_K = ATTN_WIDTH
OFF_V = 2 * ATTN_WIDTH
OFF_Z = 0
OFF_X = OFF_Z + SSD_WIDTH
OFF_B = OFF_X + SSD_WIDTH
OFF_C = OFF_B + SSD_GROUPS * SSD_STATE
OFF_DT = OFF_C + SSD_GROUPS * SSD_STATE


def _sigmoid(x):
    return 1.0 / (1.0 + jnp.exp(-x))


def _softplus(x):
    return jnp.maximum(x, 0.0) + jnp.log1p(jnp.exp(-jnp.abs(x)))


def _in_proj_kernel(pos_ref, invf_ref, x_ref, nw_ref, w_ref, qkv_ref, rest_ref,
                    hn_ref, c_ref, sa_ref, sb_ref, *, tn, n_q_tiles, n_rope_tiles, n_qkv_tiles):
    j = pl.program_id(1)

    @pl.when(j == 0)
    def _prepare():
        x = x_ref[...]
        ms = jnp.mean(x * x, axis=-1, keepdims=True)
        hn_ref[...] = (x * lax.rsqrt(ms + EPS) * nw_ref[...]).astype(BF16)
        ang = pos_ref[...].astype(F32) * invf_ref[...]
        d = lax.broadcasted_iota(I32, ang.shape, 1) % ATTN_DH
        cos = jnp.cos(ang)
        sin = jnp.sin(ang)
        c_ref[...] = jnp.where(d < ROT_DIM, cos, 1.0)
        sa_ref[...] = jnp.where(d < ROT_HALF, 0.0, jnp.where(d < ROT_DIM, sin, 0.0))
        sb_ref[...] = jnp.where(d < ROT_HALF, -sin, 0.0)

    def project():
        return lax.dot_general(hn_ref[...], w_ref[...].astype(BF16), (((1,), (1,)), ((), ())),
                               preferred_element_type=F32)

    @pl.when(j < n_rope_tiles)
    def _rope():
        acc = project()
        reps = tn // LANES
        c = jnp.tile(c_ref[...], (1, reps))
        sa = jnp.tile(sa_ref[...], (1, reps))
        sb = jnp.tile(sb_ref[...], (1, reps))
        r = acc * c + pltpu.roll(acc, ROT_HALF, 1) * sa + pltpu.roll(acc, tn - ROT_HALF, 1) * sb
        scale = jnp.where(j < n_q_tiles, ATTN_DH ** -0.5, 1.0)
        qkv_ref[...] = (r * scale).astype(qkv_ref.dtype)

    @pl.when((j >= n_rope_tiles) & (j < n_qkv_tiles))
    def _value():
        qkv_ref[...] = project().astype(qkv_ref.dtype)

    @pl.when(j >= n_qkv_tiles)
    def _rest():
        rest_ref[...] = project()


def _in_proj(x2d, pos2d, invf, norm_w, w_in_t, *, tm, tn):
    t = x2d.shape[0]
    n = w_in_t.shape[0]
    n_qkv = 3 * ATTN_WIDTH
    nqt = n_qkv // tn
    grid = (t // tm, pl.cdiv(n, tn))
    kern = functools.partial(_in_proj_kernel, tn=tn, n_q_tiles=ATTN_WIDTH // tn,
                             n_rope_tiles=2 * ATTN_WIDTH // tn, n_qkv_tiles=nqt)
    return pl.pallas_call(
        kern,
        grid=grid,
        in_specs=[
            pl.BlockSpec((tm, 1), lambda i, j: (i, 0)),
            pl.BlockSpec((1, LANES), lambda i, j: (0, 0)),
            pl.BlockSpec((tm, D_MODEL), lambda i, j: (i, 0)),
            pl.BlockSpec((1, D_MODEL), lambda i, j: (0, 0)),
            pl.BlockSpec((tn, D_MODEL), lambda i, j: (j, 0)),
        ],
        out_specs=[
            pl.BlockSpec((tm, tn), lambda i, j: (i, jnp.minimum(j, nqt - 1))),
            pl.BlockSpec((tm, tn), lambda i, j: (i, jnp.maximum(j - nqt, 0))),
        ],
        out_shape=[jax.ShapeDtypeStruct((t, n_qkv), BF16),
                   jax.ShapeDtypeStruct((t, n - n_qkv), F32)],
        scratch_shapes=[
            pltpu.VMEM((tm, D_MODEL), BF16),
            pltpu.VMEM((tm, LANES), F32),
            pltpu.VMEM((tm, LANES), F32),
            pltpu.VMEM((tm, LANES), F32),
        ],
        compiler_params=pltpu.CompilerParams(
            dimension_semantics=("parallel", "arbitrary"), vmem_limit_bytes=VMEM_LIMIT),
        name="in_proj",
    )(pos2d, invf, x2d, norm_w, w_in_t)


def _attn_kernel(lq1_ref, lk1_ref, lq2_ref, lk2_ref, q_ref, k_ref, v_ref, w_ref, o_ref, vt_ref,
                 *, tq, tk, gw, hp):
    i = pl.program_id(2)

    @pl.when(i == 0)
    def _transpose_values():
        vt_ref[...] = v_ref[...].astype(F32).T.astype(BF16)

    sub = lax.broadcasted_iota(I32, (ATTN_DV, tq), 0)
    qts = []
    for h in range(hp):
        qt = q_ref[:, h * ATTN_DV:(h + 1) * ATTN_DV].astype(F32).T
        qts.append(jnp.concatenate([jnp.where(sub < ATTN_DH, qt, 0.0),
                                    jnp.where(sub < ATTN_DH, 0.0, qt)], axis=1).astype(BF16))

    ngroups = 2 * tq // gw
    gaps = [lax.broadcasted_iota(I32, (tk, gw), 0)
            - (lax.broadcasted_iota(I32, (tk, gw), 1) + g * gw) % tq for g in range(ngroups)]
    nfull = i * (tq // tk)

    def block(n, stats, masked):
        off = pl.multiple_of(n * tk, tk)
        limit = i * tq - n * tk
        out = []
        for h in range(hp):
            kb = k_ref[pl.ds(off, tk), h * ATTN_DV:(h + 1) * ATTN_DV]
            vtb = vt_ref[h * ATTN_DV:(h + 1) * ATTN_DV, pl.ds(off, tk)]
            ss = [jnp.dot(kb, qts[h][:, g * gw:(g + 1) * gw], preferred_element_type=F32)
                  for g in range(ngroups)]
            mid = []
            for g in range(ngroups):
                m, l, _ = stats[h][g]
                s = ss[g]
                if masked:
                    s = jnp.where(gaps[g] <= limit, s, -jnp.inf)
                m_new = jnp.maximum(m, jnp.max(s, axis=0, keepdims=True))
                a = jnp.exp(m - m_new)
                p = jnp.exp(s - m_new)
                mid.append((m_new, a * l + jnp.sum(p, axis=0, keepdims=True), a, p.astype(BF16)))
            groups = []
            for g in range(ngroups):
                m_new, l, a, p = mid[g]
                acc = a * stats[h][g][2] + jnp.dot(vtb, p, preferred_element_type=F32)
                groups.append((m_new, l, acc))
            out.append(tuple(groups))
        return tuple(out)

    init = tuple(tuple((jnp.full((1, gw), -jnp.inf, F32), jnp.zeros((1, gw), F32),
                        jnp.zeros((ATTN_DV, gw), F32)) for _ in range(ngroups)) for _ in range(hp))
    bpt = tq // tk

    def several(first, stats, masked):
        for r in range(bpt):
            stats = block(first + r, stats, masked)
        return stats

    stats = lax.fori_loop(0, i, lambda t, c: several(t * bpt, c, False), init)
    stats = several(nfull, stats, True)

    lam = (jnp.exp(jnp.sum(lq1_ref[...] * lk1_ref[...], axis=-1, keepdims=True))
           - jnp.exp(jnp.sum(lq2_ref[...] * lk2_ref[...], axis=-1, keepdims=True)) + LAMBDA_INIT)
    for h in range(hp):
        l = jnp.concatenate([stats[h][g][1] for g in range(ngroups)], axis=1)
        acc = jnp.concatenate([stats[h][g][2] for g in range(ngroups)], axis=1)
        o = acc / l
        d = o[:, :tq] - lam * o[:, tq:]
        ms = jnp.mean(d * d, axis=0, keepdims=True)
        y = (d * lax.rsqrt(ms + EPS) * w_ref[...]) * (1.0 - LAMBDA_INIT)
        o_ref[:, h * ATTN_DV:(h + 1) * ATTN_DV] = y.T.astype(o_ref.dtype)


def _attention(qkv, lq1, lk1, lq2, lk2, subln_w, *, bsz, seq, tq, tk, gw, hp):
    nq = seq // tq
    bw = hp * ATTN_DV
    kb0 = OFF_K // bw
    vb0 = OFF_V // bw
    vec = pl.BlockSpec((1, ATTN_DH), lambda b, h, i: (0, 0))
    return pl.pallas_call(
        functools.partial(_attn_kernel, tq=tq, tk=tk, gw=gw, hp=hp),
        grid=(bsz, ATTN_HEADS // hp, nq),
        in_specs=[
            vec, vec, vec, vec,
            pl.BlockSpec((tq, bw), lambda b, h, i: (b * nq + i, h)),
            pl.BlockSpec((seq, bw), lambda b, h, i: (b, kb0 + h)),
            pl.BlockSpec((seq, bw), lambda b, h, i: (b, vb0 + h)),
            pl.BlockSpec((ATTN_DV, 1), lambda b, h, i: (0, 0)),
        ],
        out_specs=pl.BlockSpec((tq, bw), lambda b, h, i: (b * nq + i, h)),
        out_shape=jax.ShapeDtypeStruct((bsz * seq, ATTN_WIDTH), BF16),
        scratch_shapes=[pltpu.VMEM((bw, seq), BF16)],
        compiler_params=pltpu.CompilerParams(
            dimension_semantics=("parallel", "parallel", "arbitrary"), vmem_limit_bytes=VMEM_LIMIT),
        name="diff_attention",
    )(lq1, lk1, lq2, lk2, qkv, qkv, qkv, subln_w)


def _ssd_kernel(z_ref, xs_ref, b_ref, c_ref, dt_ref, cw_ref, cb_ref, dtb_ref, alog_ref,
                dskip_ref, nw_ref, e_ref, o_ref, ext_ref, state_ref):
    ci = pl.program_id(1)
    q = CHUNK
    gw = SSD_WIDTH // SSD_GROUPS

    @pl.when(ci == 0)
    def _init():
        ext_ref[0:SUBLANES, :] = jnp.zeros((SUBLANES, CONV_DIM), F32)
        state_ref[...] = jnp.zeros_like(state_ref)

    u = jnp.concatenate([xs_ref[...], b_ref[...], c_ref[...]], axis=1)
    ext_ref[SUBLANES:SUBLANES + q, :] = u
    acc = jnp.broadcast_to(cb_ref[...], (q, CONV_DIM))
    for w in range(CONV_W):
        acc = acc + ext_ref[pl.ds(SUBLANES - (CONV_W - 1) + w, q), :] * cw_ref[w:w + 1, :]
    ext_ref[0:SUBLANES, :] = u[q - SUBLANES:q, :]
    xbc = acc * _sigmoid(acc)
    xc = xbc[:, :SSD_WIDTH]
    bm = xbc[:, SSD_WIDTH:SSD_WIDTH + SSD_GROUPS * SSD_STATE]
    cm = xbc[:, SSD_WIDTH + SSD_GROUPS * SSD_STATE:]

    lane = lax.broadcasted_iota(I32, (q, LANES), 1)
    head_ok = lane < SSD_HEADS
    dt = jnp.where(head_ok, _softplus(jnp.where(head_ok, dt_ref[...], 0.0) + dtb_ref[...]), 0.0)
    da = dt * (-jnp.exp(alog_ref[...]))
    row = lax.broadcasted_iota(I32, (q, q), 0)
    col = lax.broadcasted_iota(I32, (q, q), 1)
    causal = row >= col
    cs = jnp.dot(causal.astype(F32), da, preferred_element_type=F32,
                 precision=lax.Precision.HIGHEST)
    cs_t = cs.T
    cs_last = cs[q - 1:q, :]

    stk = jnp.concatenate([dt, jnp.exp(cs), jnp.exp(cs_last - cs)], axis=0)
    hi = stk.astype(BF16)
    lo = (stk - hi.astype(F32)).astype(BF16)
    ex = (jnp.dot(hi, e_ref[...], preferred_element_type=F32)
          + jnp.dot(lo, e_ref[...], preferred_element_type=F32))
    dtx = ex[0:q]
    ecx = ex[q:2 * q]
    dsx = ex[2 * q:3 * q]

    xdt = xc * dtx
    xdt_b = xdt.astype(BF16)
    xw_b = (xdt * dsx).astype(BF16)
    lane_q = lax.broadcasted_iota(I32, (q, LANES), 1)

    y_groups = []
    for g in range(SSD_GROUPS):
        gs = slice(g * gw, (g + 1) * gw)
        bg = bm[:, g * SSD_STATE:(g + 1) * SSD_STATE]
        cg_b = cm[:, g * SSD_STATE:(g + 1) * SSD_STATE].astype(BF16)
        cb = lax.dot_general(cg_b, bg.astype(BF16), (((1,), (1,)), ((), ())),
                             preferred_element_type=F32)
        prev_t = state_ref[g]
        y_off = jnp.dot(cg_b, prev_t.astype(BF16), preferred_element_type=F32) * ecx[:, gs]
        st_t = jnp.dot(bg.T.astype(BF16), xw_b[:, gs], preferred_element_type=F32)
        state_ref[g] = prev_t * ecx[q - 1:q, gs] + st_t

        pairs = []
        for jp in range(gw // LANES):
            rhs = xdt_b[:, g * gw + jp * LANES: g * gw + (jp + 1) * LANES]
            outs = []
            for hh in range(2):
                h = g * (SSD_HEADS // SSD_GROUPS) + jp * 2 + hh
                seg = (jnp.broadcast_to(cs[:, h:h + 1], (q, q))
                       - jnp.broadcast_to(cs_t[h:h + 1, :], (q, q)))
                dec = jnp.exp(jnp.where(causal, seg, -jnp.inf))
                outs.append(jnp.dot((cb * dec).astype(BF16), rhs, preferred_element_type=F32))
            pairs.append(jnp.where(lane_q < SSD_HEADDIM, outs[0], outs[1]))
        y_groups.append(jnp.concatenate(pairs, axis=1) + y_off)

    y = jnp.concatenate(y_groups, axis=1) + dskip_ref[...] * xc
    z = z_ref[...]
    y = y * (z * _sigmoid(z))
    outs = []
    for g in range(SSD_GROUPS):
        yg = y[:, g * gw:(g + 1) * gw]
        outs.append(yg * lax.rsqrt(jnp.mean(yg * yg, axis=-1, keepdims=True) + EPS))
    o_ref[...] = (jnp.concatenate(outs, axis=1) * nw_ref[...]).astype(o_ref.dtype)


def _ssd(proj, conv_w, conv_b, dt_bias_p, a_log_p, dskip_x, ssd_norm_w, expand, *, bsz, seq):
    nc = seq // CHUNK
    sw = SSD_GROUPS * SSD_STATE
    const = lambda shape: pl.BlockSpec(shape, lambda b, c: (0, 0))
    return pl.pallas_call(
        _ssd_kernel,
        grid=(bsz, nc),
        in_specs=[
            pl.BlockSpec((CHUNK, SSD_WIDTH), lambda b, c: (b * nc + c, OFF_Z // SSD_WIDTH)),
            pl.BlockSpec((CHUNK, SSD_WIDTH), lambda b, c: (b * nc + c, OFF_X // SSD_WIDTH)),
            pl.BlockSpec((CHUNK, sw), lambda b, c: (b * nc + c, OFF_B // sw)),
            pl.BlockSpec((CHUNK, sw), lambda b, c: (b * nc + c, OFF_C // sw)),
            pl.BlockSpec((CHUNK, LANES), lambda b, c: (b * nc + c, OFF_DT // LANES)),
            const((CONV_W, CONV_DIM)),
            const((1, CONV_DIM)),
            const((1, LANES)),
            const((1, LANES)),
            const((1, SSD_WIDTH)),
            const((1, SSD_WIDTH)),
            const((LANES, SSD_WIDTH)),
        ],
        out_specs=pl.BlockSpec((CHUNK, SSD_WIDTH), lambda b, c: (b * nc + c, 0)),
        out_shape=jax.ShapeDtypeStruct((bsz * seq, SSD_WIDTH), BF16),
        scratch_shapes=[
            pltpu.VMEM((SUBLANES + CHUNK, CONV_DIM), F32),
            pltpu.VMEM((SSD_GROUPS, SSD_STATE, SSD_WIDTH // SSD_GROUPS), F32),
        ],
        compiler_params=pltpu.CompilerParams(
            dimension_semantics=("parallel", "arbitrary"), vmem_limit_bytes=VMEM_LIMIT),
        name="ssd",
    )(proj, proj, proj, proj, proj, conv_w, conv_b, dt_bias_p, a_log_p, dskip_x, ssd_norm_w, expand)


def _out_proj_kernel(x_ref, a_ref, s_ref, wo_ref, n2_ref, rw_ref, rb_ref, x2_ref, h2_ref, lg_ref,
                     *, sub):
    for r0 in range(0, x_ref.shape[0], sub):
        rs = slice(r0, r0 + sub)
        cat = jnp.concatenate([a_ref[rs, :], s_ref[rs, :]], axis=1)
        x2 = x_ref[rs, :] + jnp.dot(cat, wo_ref[...], preferred_element_type=F32)
        x2_ref[rs, :] = x2
        ms = jnp.mean(x2 * x2, axis=-1, keepdims=True)
        h2 = x2 * lax.rsqrt(ms + EPS) * n2_ref[...]
        h2_ref[rs, :] = h2
        h_hi = h2.astype(BF16)
        h_mid = (h2 - h_hi.astype(F32)).astype(BF16)
        p1 = jnp.dot(h_hi, rw_ref[...], preferred_element_type=F32)
        p2 = jnp.dot(h_mid, rw_ref[:, 0:LANES], preferred_element_type=F32)
        lg = p1[:, 0:LANES] + p1[:, LANES:] + p2
        lg_ref[:, rs] = lg.T[0:N_EXPERTS, :] + rb_ref[...]


def _out_proj(x2d, attn, ssd, w_out_b, norm2_w, router_w2, router_b, *, tm, sub):
    t = x2d.shape[0]
    const = lambda shape: pl.BlockSpec(shape, lambda i: (0, 0))
    return pl.pallas_call(
        functools.partial(_out_proj_kernel, sub=sub),
        grid=(t // tm,),
        in_specs=[
            pl.BlockSpec((tm, D_MODEL), lambda i: (i, 0)),
            pl.BlockSpec((tm, ATTN_WIDTH), lambda i: (i, 0)),
            pl.BlockSpec((tm, SSD_WIDTH), lambda i: (i, 0)),
            const((D_MODEL, D_MODEL)),
            const((1, D_MODEL)),
            const((D_MODEL, 2 * LANES)),
            const((N_EXPERTS, 1)),
        ],
        out_specs=[
            pl.BlockSpec((tm, D_MODEL), lambda i: (i, 0)),
            pl.BlockSpec((tm, D_MODEL), lambda i: (i, 0)),
            pl.BlockSpec((N_EXPERTS, tm), lambda i: (0, i)),
        ],
        out_shape=[
            jax.ShapeDtypeStruct((t, D_MODEL), F32),
            jax.ShapeDtypeStruct((t, D_MODEL), F32),
            jax.ShapeDtypeStruct((N_EXPERTS, t), F32),
        ],
        compiler_params=pltpu.CompilerParams(
            dimension_semantics=("parallel",), vmem_limit_bytes=VMEM_LIMIT),
        name="out_proj",
    )(x2d, attn, ssd, w_out_b, norm2_w, router_w2, router_b)


def _route_kernel(lg_ref, idx_ref, gate_ref, pos_ref, cnt_ref, off_ref, rank_ref, *, tt, row_align):
    t_total = lg_ref.shape[1]
    nt = t_total // tt
    eio = lax.broadcasted_iota(I32, (N_EXPERTS, tt), 0)
    r = lax.broadcasted_iota(I32, (tt, tt), 0)
    c = lax.broadcasted_iota(I32, (tt, tt), 1)
    before = jnp.where(r < c, 1.0, 0.0).astype(BF16)

    def tile(ti, cnt):
        off = pl.multiple_of(ti * tt, tt)
        work = lg_ref[:, pl.ds(off, tt)]
        vals, hots = [], []
        for k in range(TOP_K):
            m = jnp.max(work, axis=0, keepdims=True)
            ik = jnp.min(jnp.where(work == m, eio, N_EXPERTS), axis=0, keepdims=True)
            hot = eio == ik
            work = jnp.where(hot, -jnp.inf, work)
            idx_ref[k:k + 1, pl.ds(off, tt)] = ik
            vals.append(m)
            hots.append(hot)
        es = [jnp.exp(v - vals[0]) for v in vals]
        inv = 1.0 / (es[0] + es[1] + es[2] + es[3])
        sel = jnp.zeros((N_EXPERTS, tt), F32)
        for k in range(TOP_K):
            gate_ref[k:k + 1, pl.ds(off, tt)] = es[k] * inv
            sel = sel + jnp.where(hots[k], 1.0, 0.0)
        rank = jnp.dot(sel.astype(BF16), before, preferred_element_type=F32) + cnt
        for k in range(TOP_K):
            rank_ref[k:k + 1, pl.ds(off, tt)] = jnp.sum(jnp.where(hots[k], rank, 0.0), axis=0, keepdims=True)
        return cnt + jnp.sum(sel, axis=1, keepdims=True)

    cnt = lax.fori_loop(0, nt, tile, jnp.zeros((N_EXPERTS, 1), F32))
    cnt_ref[...] = cnt.astype(I32)
    padded = jnp.ceil(cnt * (1.0 / row_align)) * row_align
    er = lax.broadcasted_iota(I32, (N_EXPERTS, N_EXPERTS), 0)
    ec = lax.broadcasted_iota(I32, (N_EXPERTS, N_EXPERTS), 1)
    lower = jnp.where(ec < er, 1.0, 0.0)
    offs = jnp.dot(lower, jnp.broadcast_to(padded, (N_EXPERTS, LANES)), preferred_element_type=F32,
                   precision=lax.Precision.HIGHEST)[:, 0:1]
    off_ref[...] = offs.astype(I32)

    def place(ti, carry):
        off = pl.multiple_of(ti * tt, tt)
        for k in range(TOP_K):
            hot = eio == idx_ref[k:k + 1, pl.ds(off, tt)]
            base = jnp.sum(jnp.where(hot, offs, 0.0), axis=0, keepdims=True)
            pos_ref[k:k + 1, pl.ds(off, tt)] = (base + rank_ref[k:k + 1, pl.ds(off, tt)]).astype(I32)
        return carry

    lax.fori_loop(0, nt, place, 0)


def _route(logits_t, *, tt, row_align):
    t = logits_t.shape[1]
    full = lambda shape: pl.BlockSpec(shape, lambda i: (0, 0))
    return pl.pallas_call(
        functools.partial(_route_kernel, tt=tt, row_align=row_align),
        grid=(1,),
        in_specs=[full((N_EXPERTS, t))],
        out_specs=[full((TOP_K, t)), full((TOP_K, t)), full((TOP_K, t)),
                   full((N_EXPERTS, 1)), full((N_EXPERTS, 1))],
        out_shape=[
            jax.ShapeDtypeStruct((TOP_K, t), I32),
            jax.ShapeDtypeStruct((TOP_K, t), F32),
            jax.ShapeDtypeStruct((TOP_K, t), I32),
            jax.ShapeDtypeStruct((N_EXPERTS, 1), I32),
            jax.ShapeDtypeStruct((N_EXPERTS, 1), I32),
        ],
        scratch_shapes=[pltpu.VMEM((TOP_K, t), F32)],
        compiler_params=pltpu.CompilerParams(
            dimension_semantics=("arbitrary",), vmem_limit_bytes=VMEM_LIMIT),
        name="route",
    )(logits_t)


def _dispatch_kernel(pos_ref, h_ref, xs_ref, sem, *, tm, t_total):
    i = pl.program_id(0)

    def issue(tb, carry):
        for tu in range(ISSUE_UNROLL):
            t = tb * ISSUE_UNROLL + tu
            for k in range(TOP_K):
                p = pos_ref[k * t_total + i * tm + t]
                pltpu.make_async_copy(h_ref.at[pl.ds(t, 1), :], xs_ref.at[pl.ds(p, 1), :], sem).start()
        return carry

    lax.fori_loop(0, tm // ISSUE_UNROLL, issue, 0)
    for k in range(TOP_K):
        pltpu.make_async_copy(h_ref, xs_ref.at[pl.ds(0, tm), :], sem).wait()


def _dispatch(pos_flat, h2, *, tm, rows_alloc):
    t = h2.shape[0]
    return pl.pallas_call(
        functools.partial(_dispatch_kernel, tm=tm, t_total=t),
        grid_spec=pltpu.PrefetchScalarGridSpec(
            num_scalar_prefetch=1,
            grid=(t // tm,),
            in_specs=[pl.BlockSpec((tm, D_MODEL), lambda i, pos: (i, 0))],
            out_specs=pl.BlockSpec(memory_space=pl.ANY),
            scratch_shapes=[pltpu.SemaphoreType.DMA(())],
        ),
        out_shape=jax.ShapeDtypeStruct((rows_alloc, D_MODEL), F32),
        compiler_params=pltpu.CompilerParams(
            dimension_semantics=("arbitrary",), vmem_limit_bytes=VMEM_LIMIT),
        name="dispatch",
    )(pos_flat, h2)


def _experts_kernel(se_ref, ss_ref, sr_ref, sv_ref, xs_ref, wg_ref, wu_ref, wd_ref, bg_ref, bu_ref,
                    bd_ref, ys_ref, xseg, yacc, sem_in, sem_out, *, unit, nj):
    s = pl.program_id(0)
    j = pl.program_id(1)
    nseg = pl.num_programs(0)
    rows = sr_ref[s]
    start = ss_ref[s]
    sliver = unit // 4
    ahead = 4

    def plan(r):
        nu = (r + unit - 1) // unit
        return nu, (nu % 2 == 1) & (nu >= 3) & (r - (nu - 1) * unit <= sliver)

    nunits, fuse = plan(rows)
    odd = nunits % 2 == 1
    nbig = nunits // 2 - fuse.astype(I32)

    def x_copy(u):
        r0 = pl.multiple_of(start + u * unit, SUBLANES)
        l0 = pl.multiple_of(u * unit, unit)
        return pltpu.make_async_copy(xs_ref.at[pl.ds(r0, unit), :], xseg.at[pl.ds(l0, unit), :],
                                     sem_in.at[u])

    def y_copy(u, nrows=unit):
        r0 = pl.multiple_of(start + u * unit, SUBLANES)
        l0 = pl.multiple_of(u * unit, unit)
        return pltpu.make_async_copy(yacc.at[pl.ds(l0, nrows), :], ys_ref.at[pl.ds(r0, nrows), :], sem_out)

    def fetch(u):
        @pl.when(u < nunits)
        def _():
            x_copy(u).start()

    def drain(nu, fz):
        def wait_one(u, carry):
            y_copy(0).wait()
            return carry

        lax.fori_loop(0, nu - fz.astype(I32), wait_one, 0)

        @pl.when(fz)
        def _sliver():
            y_copy(0, sliver).wait()

    def chunk(u0, m):
        nu = -(-m // unit)
        l0 = pl.multiple_of(u0 * unit, unit)

        @pl.when(j == 0)
        def _arrived():
            if m == 2 * unit:
                for k in range(2):
                    fetch(u0 + ahead + k)
            for k in range(nu):
                x_copy(u0 + k).wait()

        x = xseg[pl.ds(l0, m), :].astype(BF16)
        g = jnp.dot(x, wg_ref[...].astype(BF16), preferred_element_type=F32) + bg_ref[...]
        u = jnp.dot(x, wu_ref[...].astype(BF16), preferred_element_type=F32) + bu_ref[...]
        g = jnp.minimum(g, SWIGLU_LIMIT)
        u = jnp.clip(u, -SWIGLU_LIMIT, SWIGLU_LIMIT)
        act = (g * _sigmoid(SWIGLU_ALPHA * g) * (u + 1.0)).astype(BF16)

        @pl.when(j == 0)
        def _first():
            yacc[pl.ds(l0, m), :] = bd_ref[...] + jnp.dot(
                act, wd_ref[...].astype(BF16), preferred_element_type=F32)

        @pl.when(j > 0)
        def _rest():
            yacc[pl.ds(l0, m), :] = yacc[pl.ds(l0, m), :] + jnp.dot(
                act, wd_ref[...].astype(BF16), preferred_element_type=F32)

        @pl.when(j == nj - 1)
        def _flush():
            for k in range(m // unit):
                y_copy(u0 + k).start()
            if m % unit:
                y_copy(u0 + m // unit, m % unit).start()

    @pl.when(rows > 0)
    def _work():
        @pl.when(j == 0)
        def _begin():
            @pl.when(s > 0)
            def _previous():
                drain(*plan(sr_ref[s - 1]))

            for u in range(ahead):
                fetch(u)

        def big(c, carry):
            chunk(2 * c, 2 * unit)
            return carry

        lax.fori_loop(0, nbig, big, 0)

        @pl.when(fuse)
        def _with_sliver():
            chunk(2 * nbig, 2 * unit + sliver)

        @pl.when(odd & jnp.logical_not(fuse))
        def _tail():
            chunk(2 * nbig, unit)

        is_last = (s == nseg - 1) | (sr_ref[jnp.minimum(s + 1, nseg - 1)] == 0)

        @pl.when((j == nj - 1) & is_last)
        def _final():
            drain(nunits, fuse)


def _experts(seg_e, seg_start, seg_rows, seg_valid, xs, w_gate_up, b_gate_up, w_down, b_down,
             *, unit, tf, cap, nseg):
    nj = D_FF // tf
    rows_alloc = xs.shape[0]

    def jsel(j, sv, s):
        return j * sv[s] + (nj - 1) * (1 - sv[s])

    return pl.pallas_call(
        functools.partial(_experts_kernel, unit=unit, nj=nj),
        grid_spec=pltpu.PrefetchScalarGridSpec(
            num_scalar_prefetch=4,
            grid=(nseg, nj),
            in_specs=[
                pl.BlockSpec(memory_space=pl.ANY),
                pl.BlockSpec((None, D_MODEL, tf), lambda s, j, se, ss, sr, sv: (se[s], 0, jsel(j, sv, s))),
                pl.BlockSpec((None, D_MODEL, tf), lambda s, j, se, ss, sr, sv: (se[s], 0, nj + jsel(j, sv, s))),
                pl.BlockSpec((None, tf, D_MODEL), lambda s, j, se, ss, sr, sv: (se[s], jsel(j, sv, s), 0)),
                pl.BlockSpec((None, 1, tf), lambda s, j, se, ss, sr, sv: (se[s], 0, jsel(j, sv, s))),
                pl.BlockSpec((None, 1, tf), lambda s, j, se, ss, sr, sv: (se[s], 0, nj + jsel(j, sv, s))),
                pl.BlockSpec((None, 1, D_MODEL), lambda s, j, se, ss, sr, sv: (se[s], 0, 0)),
            ],
            out_specs=pl.BlockSpec(memory_space=pl.ANY),
            scratch_shapes=[
                pltpu.VMEM((cap, D_MODEL), F32),
                pltpu.VMEM((cap, D_MODEL), F32),
                pltpu.SemaphoreType.DMA((cap // unit,)),
                pltpu.SemaphoreType.DMA(()),
            ],
        ),
        out_shape=jax.ShapeDtypeStruct((rows_alloc, D_MODEL), F32),
        compiler_params=pltpu.CompilerParams(
            dimension_semantics=("arbitrary", "arbitrary"), vmem_limit_bytes=VMEM_LIMIT),
        name="experts",
    )(seg_e, seg_start, seg_rows, seg_valid, xs, w_gate_up, w_gate_up, w_down,
      b_gate_up, b_gate_up, b_down)


def _combine_kernel(pos_ref, x2_ref, gate_ref, nf_ref, ys_ref, o_ref, buf, sem, *, tm, t_total):
    i = pl.program_id(0)
    n = pl.num_programs(0)

    def gather(tile, slot):
        def issue(tb, carry):
            for tu in range(ISSUE_UNROLL):
                t = tb * ISSUE_UNROLL + tu
                for k in range(TOP_K):
                    p = pos_ref[k * t_total + tile * tm + t]
                    pltpu.make_async_copy(ys_ref.at[pl.ds(p, 1), :], buf.at[slot, k, pl.ds(t, 1), :],
                                          sem.at[slot]).start()
            return carry

        lax.fori_loop(0, tm // ISSUE_UNROLL, issue, 0)

    @pl.when(i == 0)
    def _first():
        gather(0, 0)

    @pl.when(i + 1 < n)
    def _ahead():
        gather(i + 1, (i + 1) % 2)

    slot = i % 2
    for k in range(TOP_K):
        pltpu.make_async_copy(ys_ref.at[pl.ds(0, tm), :], buf.at[slot, k], sem.at[slot]).wait()

    g = gate_ref[...]
    x = x2_ref[...]
    for k in range(TOP_K):
        x = x + g[:, k:k + 1] * buf[slot, k]
    ms = jnp.mean(x * x, axis=-1, keepdims=True)
    o_ref[...] = x * lax.rsqrt(ms + EPS) * nf_ref[...]


def _combine(pos_flat, x2, gates_t, norm_f_w, ys, *, tm):
    t = x2.shape[0]
    return pl.pallas_call(
        functools.partial(_combine_kernel, tm=tm, t_total=t),
        grid_spec=pltpu.PrefetchScalarGridSpec(
            num_scalar_prefetch=1,
            grid=(t // tm,),
            in_specs=[
                pl.BlockSpec((tm, D_MODEL), lambda i, pos: (i, 0)),
                pl.BlockSpec((tm, TOP_K), lambda i, pos: (i, 0)),
                pl.BlockSpec((1, D_MODEL), lambda i, pos: (0, 0)),
                pl.BlockSpec(memory_space=pl.ANY),
            ],
            out_specs=pl.BlockSpec((tm, D_MODEL), lambda i, pos: (i, 0)),
            scratch_shapes=[pltpu.VMEM((2, TOP_K, tm, D_MODEL), F32), pltpu.SemaphoreType.DMA((2,))],
        ),
        out_shape=jax.ShapeDtypeStruct((t, D_MODEL), F32),
        compiler_params=pltpu.CompilerParams(
            dimension_semantics=("arbitrary",), vmem_limit_bytes=VMEM_LIMIT),
        name="combine",
    )(pos_flat, x2, gates_t, norm_f_w, ys)


def _tiles(t):
    return dict(
        in_tm=min(1024, t), in_tn=512,
        attn_tq=512, attn_tk=256, attn_gw=128, attn_hp=2,
        out_tm=min(512, t), out_sub=256,
        route_tt=min(512, t),
        disp_tm=min(256, t),
        moe_unit=256, moe_tf=512, moe_cap=min(1536, TOP_K * t),
        comb_tm=min(256, t),
    )


def _segments(cnt, offs, *, cap, nseg):
    nseg_e = (cnt + cap - 1) // cap
    ends = jnp.cumsum(nseg_e)
    total = ends[-1]
    s = jnp.arange(nseg, dtype=I32)
    valid = s < total
    s_eff = jnp.minimum(s, jnp.maximum(total - 1, 0))
    e = jnp.minimum(jnp.sum((s_eff[:, None] >= ends[None, :]).astype(I32), axis=1), N_EXPERTS - 1)
    local = s_eff - (ends[e] - nseg_e[e])
    start = offs[e] + local * cap
    rows = jnp.where(valid, jnp.clip(cnt[e] - local * cap, 0, cap), 0)
    return e, start.astype(I32), rows.astype(I32), valid.astype(I32)


def kernel(x, positions, norm1_w, w_in, lambda_q1, lambda_k1, lambda_q2, lambda_k2, attn_subln_w,
           conv_w, conv_b, dt_bias, a_log, d_skip, ssd_norm_w, w_out, norm2_w, router_w, router_b,
           w_gate_up, b_gate_up, w_down, b_down, norm_f_w):
    bsz, seq, _ = x.shape
    t = bsz * seq
    depth = w_in.shape[0]
    assert depth == 1, "single-layer block"
    cfg = _tiles(t)
    layer = 0

    x2d = x.reshape(t, D_MODEL)
    pos2d = positions.reshape(t, 1).astype(I32)
    inv_freq = ROPE_THETA ** (-jnp.arange(0, ROT_DIM, 2, dtype=F32) / ROT_DIM)
    d = jnp.arange(LANES) % ATTN_DH
    invf = jnp.where(d < ROT_DIM, inv_freq[d % ROT_HALF], 0.0).reshape(1, LANES).astype(F32)

    qkv, proj = _in_proj(x2d, pos2d, invf, norm1_w[layer].reshape(1, D_MODEL), w_in[layer].T,
                         tm=cfg["in_tm"], tn=cfg["in_tn"])

    vec = lambda a: a[layer].reshape(1, -1).astype(F32)
    attn = _attention(qkv, vec(lambda_q1), vec(lambda_k1), vec(lambda_q2), vec(lambda_k2),
                      attn_subln_w[layer].astype(F32).reshape(ATTN_DV, 1),
                      bsz=bsz, seq=seq, tq=min(cfg["attn_tq"], seq),
                      tk=min(cfg["attn_tk"], seq), gw=cfg["attn_gw"], hp=cfg["attn_hp"])

    pad_heads = lambda a: jnp.pad(a[layer].astype(F32), (0, LANES - SSD_HEADS)).reshape(1, LANES)
    head_of_lane = jnp.arange(SSD_WIDTH) // SSD_HEADDIM
    expand = (jnp.arange(LANES)[:, None] == head_of_lane[None, :]).astype(BF16)
    dskip_x = d_skip[layer].astype(F32)[head_of_lane].reshape(1, SSD_WIDTH)
    ssd = _ssd(proj, conv_w[layer], conv_b[layer].reshape(1, CONV_DIM), pad_heads(dt_bias),
               pad_heads(a_log), dskip_x, ssd_norm_w[layer].reshape(1, SSD_WIDTH), expand,
               bsz=bsz, seq=seq)

    rw = router_w[layer].astype(F32)
    rw_hi = rw.astype(BF16)
    rw_mid = (rw - rw_hi.astype(F32)).astype(BF16)
    lane_pad = lambda a: jnp.pad(a, ((0, 0), (0, LANES - N_EXPERTS)))
    router_w2 = jnp.concatenate([lane_pad(rw_hi), lane_pad(rw_mid)], axis=1)
    x2, h2, logits_t = _out_proj(
        x2d, attn, ssd, w_out[layer].astype(BF16), norm2_w[layer].reshape(1, D_MODEL),
        router_w2, router_b[layer].reshape(N_EXPERTS, 1),
        tm=cfg["out_tm"], sub=min(cfg["out_sub"], cfg["out_tm"]))

    idx, gates, pos, cnt, offs = _route(logits_t, tt=cfg["route_tt"], row_align=SUBLANES)
    del idx
    pos_flat = pos.reshape(TOP_K * t)

    cap, unit = cfg["moe_cap"], cfg["moe_unit"]
    nseg = N_EXPERTS + (TOP_K * t) // cap
    rows_alloc = TOP_K * t + N_EXPERTS * SUBLANES + unit
    seg_e, seg_start, seg_rows, seg_valid = _segments(
        cnt.reshape(N_EXPERTS), offs.reshape(N_EXPERTS), cap=cap, nseg=nseg)

    xs = _dispatch(pos_flat, h2, tm=cfg["disp_tm"], rows_alloc=rows_alloc)
    ys = _experts(seg_e, seg_start, seg_rows, seg_valid, xs, w_gate_up[layer],
                  b_gate_up[layer].reshape(N_EXPERTS, 1, 2 * D_FF), w_down[layer],
                  b_down[layer].reshape(N_EXPERTS, 1, D_MODEL),
                  unit=unit, tf=cfg["moe_tf"], cap=cap, nseg=nseg)
    out = _combine(pos_flat, x2, gates.T, norm_f_w.reshape(1, D_MODEL), ys, tm=cfg["comb_tm"])
    return out.reshape(bsz, seq, D_MODEL)
```

```python
import functools
import math

import jax
import jax.numpy as jnp
from jax import lax
from jax.experimental import pallas as pl
from jax.experimental.pallas import tpu as pltpu

F32 = jnp.float32
BF16 = jnp.bfloat16
I32 = jnp.int32

D_MODEL = 2048
ATTN_WIDTH = 1024
SSD_WIDTH = 1024
ATTN_HEADS = 8
ATTN_DH = 64
ATTN_DV = 128
ROT_DIM = 16
ROT_HALF = ROT_DIM // 2
ROPE_THETA = 500000.0
SSD_HEADDIM = 64
SSD_HEADS = 16
SSD_GROUPS = 2
SSD_STATE = 128
CONV_W = 4
CONV_DIM = SSD_WIDTH + 2 * SSD_GROUPS * SSD_STATE
CHUNK = 128
N_EXPERTS = 32
TOP_K = 4
D_FF = 2048
SWIGLU_ALPHA = 1.702
SWIGLU_LIMIT = 7.0
EPS = 1e-5
IN_TOTAL = 3 * ATTN_WIDTH + SSD_WIDTH + CONV_DIM + SSD_HEADS
LAMBDA_INIT = 0.8 - 0.6 * math.exp(-0.3 * 0)

LANES = 128
SUBLANES = 8
VMEM_LIMIT = 56 * 1024 * 1024
ISSUE_UNROLL = 4

OFF_Q = 0
OFF_K = ATTN_WIDTH
OFF_V = 2 * ATTN_WIDTH
OFF_Z = 0
OFF_X = OFF_Z + SSD_WIDTH
OFF_B = OFF_X + SSD_WIDTH
OFF_C = OFF_B + SSD_GROUPS * SSD_STATE
OFF_DT = OFF_C + SSD_GROUPS * SSD_STATE


def _sigmoid(x):
    return 1.0 / (1.0 + jnp.exp(-x))


def _softplus(x):
    return jnp.maximum(x, 0.0) + jnp.log1p(jnp.exp(-jnp.abs(x)))


def _in_proj_kernel(pos_ref, invf_ref, x_ref, nw_ref, w_ref, qkv_ref, rest_ref,
                    hn_ref, c_ref, sa_ref, sb_ref, *, tn, n_q_tiles, n_rope_tiles, n_qkv_tiles):
    j = pl.program_id(1)

    @pl.when(j == 0)
    def _prepare():
        x = x_ref[...]
        ms = jnp.mean(x * x, axis=-1, keepdims=True)
        hn_ref[...] = (x * lax.rsqrt(ms + EPS) * nw_ref[...]).astype(BF16)
        ang = pos_ref[...].astype(F32) * invf_ref[...]
        d = lax.broadcasted_iota(I32, ang.shape, 1) % ATTN_DH
        cos = jnp.cos(ang)
        sin = jnp.sin(ang)
        c_ref[...] = jnp.where(d < ROT_DIM, cos, 1.0)
        sa_ref[...] = jnp.where(d < ROT_HALF, 0.0, jnp.where(d < ROT_DIM, sin, 0.0))
        sb_ref[...] = jnp.where(d < ROT_HALF, -sin, 0.0)

    def project():
        return lax.dot_general(hn_ref[...], w_ref[...].astype(BF16), (((1,), (1,)), ((), ())),
                               preferred_element_type=F32)

    @pl.when(j < n_rope_tiles)
    def _rope():
        acc = project()
        reps = tn // LANES
        c = jnp.tile(c_ref[...], (1, reps))
        sa = jnp.tile(sa_ref[...], (1, reps))
        sb = jnp.tile(sb_ref[...], (1, reps))
        r = acc * c + pltpu.roll(acc, ROT_HALF, 1) * sa + pltpu.roll(acc, tn - ROT_HALF, 1) * sb
        scale = jnp.where(j < n_q_tiles, ATTN_DH ** -0.5, 1.0)
        qkv_ref[...] = (r * scale).astype(qkv_ref.dtype)

    @pl.when((j >= n_rope_tiles) & (j < n_qkv_tiles))
    def _value():
        qkv_ref[...] = project().astype(qkv_ref.dtype)

    @pl.when(j >= n_qkv_tiles)
    def _rest():
        rest_ref[...] = project()


def _in_proj(x2d, pos2d, invf, norm_w, w_in_t, *, tm, tn):
    t = x2d.shape[0]
    n = w_in_t.shape[0]
    n_qkv = 3 * ATTN_WIDTH
    nqt = n_qkv // tn
    grid = (t // tm, pl.cdiv(n, tn))
    kern = functools.partial(_in_proj_kernel, tn=tn, n_q_tiles=ATTN_WIDTH // tn,
                             n_rope_tiles=2 * ATTN_WIDTH // tn, n_qkv_tiles=nqt)
    return pl.pallas_call(
        kern,
        grid=grid,
        in_specs=[
            pl.BlockSpec((tm, 1), lambda i, j: (i, 0)),
            pl.BlockSpec((1, LANES), lambda i, j: (0, 0)),
            pl.BlockSpec((tm, D_MODEL), lambda i, j: (i, 0)),
            pl.BlockSpec((1, D_MODEL), lambda i, j: (0, 0)),
            pl.BlockSpec((tn, D_MODEL), lambda i, j: (j, 0)),
        ],
        out_specs=[
            pl.BlockSpec((tm, tn), lambda i, j: (i, jnp.minimum(j, nqt - 1))),
            pl.BlockSpec((tm, tn), lambda i, j: (i, jnp.maximum(j - nqt, 0))),
        ],
        out_shape=[jax.ShapeDtypeStruct((t, n_qkv), BF16),
                   jax.ShapeDtypeStruct((t, n - n_qkv), F32)],
        scratch_shapes=[
            pltpu.VMEM((tm, D_MODEL), BF16),
            pltpu.VMEM((tm, LANES), F32),
            pltpu.VMEM((tm, LANES), F32),
            pltpu.VMEM((tm, LANES), F32),
        ],
        compiler_params=pltpu.CompilerParams(
            dimension_semantics=("parallel", "arbitrary"), vmem_limit_bytes=VMEM_LIMIT),
        name="in_proj",
    )(pos2d, invf, x2d, norm_w, w_in_t)


def _attn_kernel(lq1_ref, lk1_ref, lq2_ref, lk2_ref, q_ref, k_ref, v_ref, w_ref, o_ref, vt_ref,
                 *, tq, tk, gw, hp):
    i = pl.program_id(2)

    @pl.when(i == 0)
    def _transpose_values():
        vt_ref[...] = v_ref[...].astype(F32).T.astype(BF16)

    sub = lax.broadcasted_iota(I32, (ATTN_DV, tq), 0)
    qts = []
    for h in range(hp):
        qt = q_ref[:, h * ATTN_DV:(h + 1) * ATTN_DV].astype(F32).T
        qts.append(jnp.concatenate([jnp.where(sub < ATTN_DH, qt, 0.0),
                                    jnp.where(sub < ATTN_DH, 0.0, qt)], axis=1).astype(BF16))

    ngroups = 2 * tq // gw
    gaps = [lax.broadcasted_iota(I32, (tk, gw), 0)
            - (lax.broadcasted_iota(I32, (tk, gw), 1) + g * gw) % tq for g in range(ngroups)]
    nfull = i * (tq // tk)

    def block(n, stats, diag):
        off = pl.multiple_of(n * tk, tk)
        limit = i * tq - n * tk
        live = [g for g in range(ngroups)
                if diag is None or (g * gw) % tq + gw > diag * tk]
        out = []
        for h in range(hp):
            kb = k_ref[pl.ds(off, tk), h * ATTN_DV:(h + 1) * ATTN_DV]
            vtb = vt_ref[h * ATTN_DV:(h + 1) * ATTN_DV, pl.ds(off, tk)]
            ss = {g: jnp.dot(kb, qts[h][:, g * gw:(g + 1) * gw], preferred_element_type=F32)
                  for g in live}
            mid = {}
            for g in live:
                m, l, _ = stats[h][g]
                s = ss[g]
                if diag is not None and (g * gw) % tq < diag * tk + tk - 1:
                    s = jnp.where(gaps[g] <= limit, s, -jnp.inf)
                m_new = jnp.maximum(m, jnp.max(s, axis=0, keepdims=True))
                a = jnp.exp(m - m_new)
                p = jnp.exp(s - m_new)
                mid[g] = (m_new, a * l + jnp.sum(p, axis=0, keepdims=True), a, p.astype(BF16))
            groups = []
            for g in range(ngroups):
                if g not in mid:
                    groups.append(stats[h][g])
                    continue
                m_new, l, a, p = mid[g]
                acc = a * stats[h][g][2] + jnp.dot(vtb, p, preferred_element_type=F32)
                groups.append((m_new, l, acc))
            out.append(tuple(groups))
        return tuple(out)

    init = tuple(tuple((jnp.full((1, gw), -jnp.inf, F32), jnp.zeros((1, gw), F32),
                        jnp.zeros((ATTN_DV, gw), F32)) for _ in range(ngroups)) for _ in range(hp))
    bpt = tq // tk

    def several(first, stats, masked):
        for r in range(bpt):
            stats = block(first + r, stats, r if masked else None)
        return stats

    stats = lax.fori_loop(0, i, lambda t, c: several(t * bpt, c, False), init)
    stats = several(nfull, stats, True)

    lam = (jnp.exp(jnp.sum(lq1_ref[...] * lk1_ref[...], axis=-1, keepdims=True))
           - jnp.exp(jnp.sum(lq2_ref[...] * lk2_ref[...], axis=-1, keepdims=True)) + LAMBDA_INIT)
    for h in range(hp):
        l = jnp.concatenate([stats[h][g][1] for g in range(ngroups)], axis=1)
        acc = jnp.concatenate([stats[h][g][2] for g in range(ngroups)], axis=1)
        o = acc / l
        d = o[:, :tq] - lam * o[:, tq:]
        ms = jnp.mean(d * d, axis=0, keepdims=True)
        y = (d * lax.rsqrt(ms + EPS) * w_ref[...]) * (1.0 - LAMBDA_INIT)
        o_ref[:, h * ATTN_DV:(h + 1) * ATTN_DV] = y.T.astype(o_ref.dtype)


def _attention(qkv, lq1, lk1, lq2, lk2, subln_w, *, bsz, seq, tq, tk, gw, hp):
    nq = seq // tq
    bw = hp * ATTN_DV
    kb0 = OFF_K // bw
    vb0 = OFF_V // bw
    vec = pl.BlockSpec((1, ATTN_DH), lambda b, h, i: (0, 0))
    return pl.pallas_call(
        functools.partial(_attn_kernel, tq=tq, tk=tk, gw=gw, hp=hp),
        grid=(bsz, ATTN_HEADS // hp, nq),
        in_specs=[
            vec, vec, vec, vec,
            pl.BlockSpec((tq, bw), lambda b, h, i: (b * nq + i, h)),
            pl.BlockSpec((seq, bw), lambda b, h, i: (b, kb0 + h)),
            pl.BlockSpec((seq, bw), lambda b, h, i: (b, vb0 + h)),
            pl.BlockSpec((ATTN_DV, 1), lambda b, h, i: (0, 0)),
        ],
        out_specs=pl.BlockSpec((tq, bw), lambda b, h, i: (b * nq + i, h)),
        out_shape=jax.ShapeDtypeStruct((bsz * seq, ATTN_WIDTH), BF16),
        scratch_shapes=[pltpu.VMEM((bw, seq), BF16)],
        compiler_params=pltpu.CompilerParams(
            dimension_semantics=("parallel", "parallel", "arbitrary"), vmem_limit_bytes=VMEM_LIMIT),
        name="diff_attention",
    )(lq1, lk1, lq2, lk2, qkv, qkv, qkv, subln_w)


def _ssd_kernel(z_ref, xs_ref, b_ref, c_ref, dt_ref, cw_ref, cb_ref, dtb_ref, alog_ref,
                dskip_ref, nw_ref, e_ref, o_ref, ext_ref, state_ref):
    ci = pl.program_id(1)
    q = CHUNK
    gw = SSD_WIDTH // SSD_GROUPS

    @pl.when(ci == 0)
    def _init():
        ext_ref[0:SUBLANES, :] = jnp.zeros((SUBLANES, CONV_DIM), F32)
        state_ref[...] = jnp.zeros_like(state_ref)

    u = jnp.concatenate([xs_ref[...], b_ref[...], c_ref[...]], axis=1)
    ext_ref[SUBLANES:SUBLANES + q, :] = u
    acc = jnp.broadcast_to(cb_ref[...], (q, CONV_DIM))
    for w in range(CONV_W):
        acc = acc + ext_ref[pl.ds(SUBLANES - (CONV_W - 1) + w, q), :] * cw_ref[w:w + 1, :]
    ext_ref[0:SUBLANES, :] = u[q - SUBLANES:q, :]
    xbc = acc * _sigmoid(acc)
    xc = xbc[:, :SSD_WIDTH]
    bm = xbc[:, SSD_WIDTH:SSD_WIDTH + SSD_GROUPS * SSD_STATE]
    cm = xbc[:, SSD_WIDTH + SSD_GROUPS * SSD_STATE:]

    lane = lax.broadcasted_iota(I32, (q, LANES), 1)
    head_ok = lane < SSD_HEADS
    dt = jnp.where(head_ok, _softplus(jnp.where(head_ok, dt_ref[...], 0.0) + dtb_ref[...]), 0.0)
    da = dt * (-jnp.exp(alog_ref[...]))
    row = lax.broadcasted_iota(I32, (q, q), 0)
    col = lax.broadcasted_iota(I32, (q, q), 1)
    causal = row >= col
    cs = jnp.dot(causal.astype(F32), da, preferred_element_type=F32,
                 precision=lax.Precision.HIGHEST)
    cs_t = cs.T
    cs_last = cs[q - 1:q, :]

    stk = jnp.concatenate([dt, jnp.exp(cs), jnp.exp(cs_last - cs)], axis=0)
    hi = stk.astype(BF16)
    lo = (stk - hi.astype(F32)).astype(BF16)
    ex = (jnp.dot(hi, e_ref[...], preferred_element_type=F32)
          + jnp.dot(lo, e_ref[...], preferred_element_type=F32))
    dtx = ex[0:q]
    ecx = ex[q:2 * q]
    dsx = ex[2 * q:3 * q]

    xdt = xc * dtx
    xdt_b = xdt.astype(BF16)
    xw_b = (xdt * dsx).astype(BF16)
    lane_q = lax.broadcasted_iota(I32, (q, LANES), 1)

    y_groups = []
    for g in range(SSD_GROUPS):
        gs = slice(g * gw, (g + 1) * gw)
        bg = bm[:, g * SSD_STATE:(g + 1) * SSD_STATE]
        cg_b = cm[:, g * SSD_STATE:(g + 1) * SSD_STATE].astype(BF16)
        cb = lax.dot_general(cg_b, bg.astype(BF16), (((1,), (1,)), ((), ())),
                             preferred_element_type=F32)
        prev_t = state_ref[g]
        y_off = jnp.dot(cg_b, prev_t.astype(BF16), preferred_element_type=F32) * ecx[:, gs]
        st_t = jnp.dot(bg.T.astype(BF16), xw_b[:, gs], preferred_element_type=F32)
        state_ref[g] = prev_t * ecx[q - 1:q, gs] + st_t

        pairs = []
        for jp in range(gw // LANES):
            rhs = xdt_b[:, g * gw + jp * LANES: g * gw + (jp + 1) * LANES]
            outs = []
            for hh in range(2):
                h = g * (SSD_HEADS // SSD_GROUPS) + jp * 2 + hh
                seg = (jnp.broadcast_to(cs[:, h:h + 1], (q, q))
                       - jnp.broadcast_to(cs_t[h:h + 1, :], (q, q)))
                dec = jnp.exp(jnp.where(causal, seg, -jnp.inf))
                outs.append(jnp.dot((cb * dec).astype(BF16), rhs, preferred_element_type=F32))
            pairs.append(jnp.where(lane_q < SSD_HEADDIM, outs[0], outs[1]))
        y_groups.append(jnp.concatenate(pairs, axis=1) + y_off)

    y = jnp.concatenate(y_groups, axis=1) + dskip_ref[...] * xc
    z = z_ref[...]
    y = y * (z * _sigmoid(z))
    outs = []
    for g in range(SSD_GROUPS):
        yg = y[:, g * gw:(g + 1) * gw]
        outs.append(yg * lax.rsqrt(jnp.mean(yg * yg, axis=-1, keepdims=True) + EPS))
    o_ref[...] = (jnp.concatenate(outs, axis=1) * nw_ref[...]).astype(o_ref.dtype)


def _ssd(proj, conv_w, conv_b, dt_bias_p, a_log_p, dskip_x, ssd_norm_w, expand, *, bsz, seq):
    nc = seq // CHUNK
    sw = SSD_GROUPS * SSD_STATE
    const = lambda shape: pl.BlockSpec(shape, lambda b, c: (0, 0))
    return pl.pallas_call(
        _ssd_kernel,
        grid=(bsz, nc),
        in_specs=[
            pl.BlockSpec((CHUNK, SSD_WIDTH), lambda b, c: (b * nc + c, OFF_Z // SSD_WIDTH)),
            pl.BlockSpec((CHUNK, SSD_WIDTH), lambda b, c: (b * nc + c, OFF_X // SSD_WIDTH)),
            pl.BlockSpec((CHUNK, sw), lambda b, c: (b * nc + c, OFF_B // sw)),
            pl.BlockSpec((CHUNK, sw), lambda b, c: (b * nc + c, OFF_C // sw)),
            pl.BlockSpec((CHUNK, LANES), lambda b, c: (b * nc + c, OFF_DT // LANES)),
            const((CONV_W, CONV_DIM)),
            const((1, CONV_DIM)),
            const((1, LANES)),
            const((1, LANES)),
            const((1, SSD_WIDTH)),
            const((1, SSD_WIDTH)),
            const((LANES, SSD_WIDTH)),
        ],
        out_specs=pl.BlockSpec((CHUNK, SSD_WIDTH), lambda b, c: (b * nc + c, 0)),
        out_shape=jax.ShapeDtypeStruct((bsz * seq, SSD_WIDTH), BF16),
        scratch_shapes=[
            pltpu.VMEM((SUBLANES + CHUNK, CONV_DIM), F32),
            pltpu.VMEM((SSD_GROUPS, SSD_STATE, SSD_WIDTH // SSD_GROUPS), F32),
        ],
        compiler_params=pltpu.CompilerParams(
            dimension_semantics=("parallel", "arbitrary"), vmem_limit_bytes=VMEM_LIMIT),
        name="ssd",
    )(proj, proj, proj, proj, proj, conv_w, conv_b, dt_bias_p, a_log_p, dskip_x, ssd_norm_w, expand)


def _out_proj_kernel(x_ref, a_ref, s_ref, wo_ref, n2_ref, rw_ref, rb_ref, x2_ref, h2_ref, lg_ref,
                     *, sub):
    for r0 in range(0, x_ref.shape[0], sub):
        rs = slice(r0, r0 + sub)
        cat = jnp.concatenate([a_ref[rs, :], s_ref[rs, :]], axis=1)
        x2 = x_ref[rs, :] + jnp.dot(cat, wo_ref[...], preferred_element_type=F32)
        x2_ref[rs, :] = x2
        ms = jnp.mean(x2 * x2, axis=-1, keepdims=True)
        h2 = x2 * lax.rsqrt(ms + EPS) * n2_ref[...]
        h2_ref[rs, :] = h2
        h_hi = h2.astype(BF16)
        h_mid = (h2 - h_hi.astype(F32)).astype(BF16)
        p1 = jnp.dot(h_hi, rw_ref[...], preferred_element_type=F32)
        p2 = jnp.dot(h_mid, rw_ref[:, 0:LANES], preferred_element_type=F32)
        lg = p1[:, 0:LANES] + p1[:, LANES:] + p2
        lg_ref[:, rs] = lg.T[0:N_EXPERTS, :] + rb_ref[...]


def _out_proj(x2d, attn, ssd, w_out_b, norm2_w, router_w2, router_b, *, tm, sub):
    t = x2d.shape[0]
    const = lambda shape: pl.BlockSpec(shape, lambda i: (0, 0))
    return pl.pallas_call(
        functools.partial(_out_proj_kernel, sub=sub),
        grid=(t // tm,),
        in_specs=[
            pl.BlockSpec((tm, D_MODEL), lambda i: (i, 0)),
            pl.BlockSpec((tm, ATTN_WIDTH), lambda i: (i, 0)),
            pl.BlockSpec((tm, SSD_WIDTH), lambda i: (i, 0)),
            const((D_MODEL, D_MODEL)),
            const((1, D_MODEL)),
            const((D_MODEL, 2 * LANES)),
            const((N_EXPERTS, 1)),
        ],
        out_specs=[
            pl.BlockSpec((tm, D_MODEL), lambda i: (i, 0)),
            pl.BlockSpec((tm, D_MODEL), lambda i: (i, 0)),
            pl.BlockSpec((N_EXPERTS, tm), lambda i: (0, i)),
        ],
        out_shape=[
            jax.ShapeDtypeStruct((t, D_MODEL), F32),
            jax.ShapeDtypeStruct((t, D_MODEL), F32),
            jax.ShapeDtypeStruct((N_EXPERTS, t), F32),
        ],
        compiler_params=pltpu.CompilerParams(
            dimension_semantics=("parallel",), vmem_limit_bytes=VMEM_LIMIT),
        name="out_proj",
    )(x2d, attn, ssd, w_out_b, norm2_w, router_w2, router_b)


def _route_kernel(lg_ref, idx_ref, gate_ref, pos_ref, cnt_ref, off_ref, rank_ref, *, tt, row_align):
    t_total = lg_ref.shape[1]
    nt = t_total // tt
    eio = lax.broadcasted_iota(I32, (N_EXPERTS, tt), 0)
    r = lax.broadcasted_iota(I32, (tt, tt), 0)
    c = lax.broadcasted_iota(I32, (tt, tt), 1)
    before = jnp.where(r < c, 1.0, 0.0).astype(BF16)

    def tile(ti, cnt):
        off = pl.multiple_of(ti * tt, tt)
        work = lg_ref[:, pl.ds(off, tt)]
        vals, hots = [], []
        for k in range(TOP_K):
            m = jnp.max(work, axis=0, keepdims=True)
            ik = jnp.min(jnp.where(work == m, eio, N_EXPERTS), axis=0, keepdims=True)
            hot = eio == ik
            work = jnp.where(hot, -jnp.inf, work)
            idx_ref[k:k + 1, pl.ds(off, tt)] = ik
            vals.append(m)
            hots.append(hot)
        es = [jnp.exp(v - vals[0]) for v in vals]
        inv = 1.0 / (es[0] + es[1] + es[2] + es[3])
        sel = jnp.zeros((N_EXPERTS, tt), F32)
        for k in range(TOP_K):
            gate_ref[k:k + 1, pl.ds(off, tt)] = es[k] * inv
            sel = sel + jnp.where(hots[k], 1.0, 0.0)
        rank = jnp.dot(sel.astype(BF16), before, preferred_element_type=F32) + cnt
        for k in range(TOP_K):
            rank_ref[k:k + 1, pl.ds(off, tt)] = jnp.sum(jnp.where(hots[k], rank, 0.0), axis=0, keepdims=True)
        return cnt + jnp.sum(sel, axis=1, keepdims=True)

    cnt = lax.fori_loop(0, nt, tile, jnp.zeros((N_EXPERTS, 1), F32))
    cnt_ref[...] = cnt.astype(I32)
    padded = jnp.ceil(cnt * (1.0 / row_align)) * row_align
    er = lax.broadcasted_iota(I32, (N_EXPERTS, N_EXPERTS), 0)
    ec = lax.broadcasted_iota(I32, (N_EXPERTS, N_EXPERTS), 1)
    lower = jnp.where(ec < er, 1.0, 0.0)
    offs = jnp.dot(lower, jnp.broadcast_to(padded, (N_EXPERTS, LANES)), preferred_element_type=F32,
                   precision=lax.Precision.HIGHEST)[:, 0:1]
    off_ref[...] = offs.astype(I32)

    def place(ti, carry):
        off = pl.multiple_of(ti * tt, tt)
        for k in range(TOP_K):
            hot = eio == idx_ref[k:k + 1, pl.ds(off, tt)]
            base = jnp.sum(jnp.where(hot, offs, 0.0), axis=0, keepdims=True)
            pos_ref[k:k + 1, pl.ds(off, tt)] = (base + rank_ref[k:k + 1, pl.ds(off, tt)]).astype(I32)
        return carry

    lax.fori_loop(0, nt, place, 0)


def _route(logits_t, *, tt, row_align):
    t = logits_t.shape[1]
    full = lambda shape: pl.BlockSpec(shape, lambda i: (0, 0))
    return pl.pallas_call(
        functools.partial(_route_kernel, tt=tt, row_align=row_align),
        grid=(1,),
        in_specs=[full((N_EXPERTS, t))],
        out_specs=[full((TOP_K, t)), full((TOP_K, t)), full((TOP_K, t)),
                   full((N_EXPERTS, 1)), full((N_EXPERTS, 1))],
        out_shape=[
            jax.ShapeDtypeStruct((TOP_K, t), I32),
            jax.ShapeDtypeStruct((TOP_K, t), F32),
            jax.ShapeDtypeStruct((TOP_K, t), I32),
            jax.ShapeDtypeStruct((N_EXPERTS, 1), I32),
            jax.ShapeDtypeStruct((N_EXPERTS, 1), I32),
        ],
        scratch_shapes=[pltpu.VMEM((TOP_K, t), F32)],
        compiler_params=pltpu.CompilerParams(
            dimension_semantics=("arbitrary",), vmem_limit_bytes=VMEM_LIMIT),
        name="route",
    )(logits_t)


def _dispatch_kernel(pos_ref, h_ref, xs_ref, sem, *, tm, t_total):
    i = pl.program_id(0)

    def issue(tb, carry):
        for tu in range(ISSUE_UNROLL):
            t = tb * ISSUE_UNROLL + tu
            for k in range(TOP_K):
                p = pos_ref[k * t_total + i * tm + t]
                pltpu.make_async_copy(h_ref.at[pl.ds(t, 1), :], xs_ref.at[pl.ds(p, 1), :], sem).start()
        return carry

    lax.fori_loop(0, tm // ISSUE_UNROLL, issue, 0)
    for k in range(TOP_K):
        pltpu.make_async_copy(h_ref, xs_ref.at[pl.ds(0, tm), :], sem).wait()


def _dispatch(pos_flat, h2, *, tm, rows_alloc):
    t = h2.shape[0]
    return pl.pallas_call(
        functools.partial(_dispatch_kernel, tm=tm, t_total=t),
        grid_spec=pltpu.PrefetchScalarGridSpec(
            num_scalar_prefetch=1,
            grid=(t // tm,),
            in_specs=[pl.BlockSpec((tm, D_MODEL), lambda i, pos: (i, 0))],
            out_specs=pl.BlockSpec(memory_space=pl.ANY),
            scratch_shapes=[pltpu.SemaphoreType.DMA(())],
        ),
        out_shape=jax.ShapeDtypeStruct((rows_alloc, D_MODEL), F32),
        compiler_params=pltpu.CompilerParams(
            dimension_semantics=("arbitrary",), vmem_limit_bytes=VMEM_LIMIT),
        name="dispatch",
    )(pos_flat, h2)


def _experts_kernel(se_ref, ss_ref, sr_ref, sv_ref, xs_ref, wg_ref, wu_ref, wd_ref, bg_ref, bu_ref,
                    bd_ref, ys_ref, xseg, yacc, sem_in, sem_out, *, unit, max_units, nj):
    s = pl.program_id(0)
    j = pl.program_id(1)
    nseg = pl.num_programs(0)
    rows = sr_ref[s]
    start = ss_ref[s]
    sliver = unit // 4
    ahead = 4

    def plan(r):
        nu = (r + unit - 1) // unit
        return nu, (nu % 2 == 1) & (nu >= 3) & (r - (nu - 1) * unit <= sliver)

    nunits, fuse = plan(rows)
    odd = nunits % 2 == 1
    nbig = nunits // 2 - fuse.astype(I32)
    prev_units, prev_fuse = plan(sr_ref[jnp.maximum(s - 1, 0)])
    prev_units = jnp.where(s > 0, prev_units, 0)
    nxt = jnp.minimum(s + 1, nseg - 1)
    has_next = (s + 1 < nseg) & (sr_ref[nxt] > 0)
    next_units = jnp.where(has_next, plan(sr_ref[nxt])[0], 0)

    def x_copy(u, base):
        r0 = pl.multiple_of(base + u * unit, SUBLANES)
        l0 = pl.multiple_of(u * unit, unit)
        return pltpu.make_async_copy(xs_ref.at[pl.ds(r0, unit), :], xseg.at[pl.ds(l0, unit), :],
                                     sem_in.at[u])

    def y_copy(u, nrows=unit):
        r0 = pl.multiple_of(start + u * unit, SUBLANES)
        l0 = pl.multiple_of(u * unit, unit)
        return pltpu.make_async_copy(yacc.at[pl.ds(l0, nrows), :], ys_ref.at[pl.ds(r0, nrows), :],
                                     sem_out.at[u])

    def fetch(u):
        @pl.when((u < nunits) & (u >= prev_units))
        def _():
            x_copy(u, start).start()

    def await_copy_out(u, n_units, fused):
        @pl.when((u < n_units) & jnp.logical_not(fused & (u == n_units - 1)))
        def _whole():
            y_copy(u).wait()

        @pl.when((u < n_units) & fused & (u == n_units - 1))
        def _part():
            y_copy(u, sliver).wait()

    def chunk(u0, m):
        nu = -(-m // unit)
        l0 = pl.multiple_of(u0 * unit, unit)

        @pl.when(j == 0)
        def _arrived():
            if m == 2 * unit:
                for k in range(2):
                    fetch(u0 + ahead + k)
            for k in range(nu):
                x_copy(u0 + k, start).wait()
                await_copy_out(u0 + k, prev_units, prev_fuse)

        x = xseg[pl.ds(l0, m), :].astype(BF16)
        g = jnp.dot(x, wg_ref[...].astype(BF16), preferred_element_type=F32) + bg_ref[...]
        u = jnp.dot(x, wu_ref[...].astype(BF16), preferred_element_type=F32) + bu_ref[...]
        g = jnp.minimum(g, SWIGLU_LIMIT)
        u = jnp.clip(u, -SWIGLU_LIMIT, SWIGLU_LIMIT)
        act = (g * _sigmoid(SWIGLU_ALPHA * g) * (u + 1.0)).astype(BF16)

        @pl.when(j == 0)
        def _first():
            yacc[pl.ds(l0, m), :] = bd_ref[...] + jnp.dot(
                act, wd_ref[...].astype(BF16), preferred_element_type=F32)

        @pl.when(j > 0)
        def _rest():
            yacc[pl.ds(l0, m), :] = yacc[pl.ds(l0, m), :] + jnp.dot(
                act, wd_ref[...].astype(BF16), preferred_element_type=F32)

        @pl.when(j == nj - 1)
        def _flush():
            for k in range(m // unit):
                y_copy(u0 + k).start()
            if m % unit:
                y_copy(u0 + m // unit, m % unit).start()
            for k in range(nu):
                @pl.when(u0 + k < next_units)
                def _():
                    x_copy(u0 + k, ss_ref[nxt]).start()

    @pl.when(rows > 0)
    def _work():
        @pl.when(j == 0)
        def _begin():
            for u in range(max_units):
                @pl.when(u >= nunits)
                def _():
                    await_copy_out(u, prev_units, prev_fuse)

            for u in range(ahead):
                fetch(u)

        def big(c, carry):
            chunk(2 * c, 2 * unit)
            return carry

        lax.fori_loop(0, nbig, big, 0)

        @pl.when(fuse)
        def _with_sliver():
            chunk(2 * nbig, 2 * unit + sliver)

        @pl.when(odd & jnp.logical_not(fuse))
        def _tail():
            chunk(2 * nbig, unit)

        @pl.when((j == nj - 1) & jnp.logical_not(has_next))
        def _final():
            for u in range(max_units):
                await_copy_out(u, nunits, fuse)


def _experts(seg_e, seg_start, seg_rows, seg_valid, xs, w_gate_up, b_gate_up, w_down, b_down,
             *, unit, tf, cap, nseg):
    nj = D_FF // tf
    rows_alloc = xs.shape[0]

    def jsel(j, sv, s):
        return j * sv[s] + (nj - 1) * (1 - sv[s])

    return pl.pallas_call(
        functools.partial(_experts_kernel, unit=unit, max_units=cap // unit, nj=nj),
        grid_spec=pltpu.PrefetchScalarGridSpec(
            num_scalar_prefetch=4,
            grid=(nseg, nj),
            in_specs=[
                pl.BlockSpec(memory_space=pl.ANY),
                pl.BlockSpec((None, D_MODEL, tf), lambda s, j, se, ss, sr, sv: (se[s], 0, jsel(j, sv, s))),
                pl.BlockSpec((None, D_MODEL, tf), lambda s, j, se, ss, sr, sv: (se[s], 0, nj + jsel(j, sv, s))),
                pl.BlockSpec((None, tf, D_MODEL), lambda s, j, se, ss, sr, sv: (se[s], jsel(j, sv, s), 0)),
                pl.BlockSpec((None, 1, tf), lambda s, j, se, ss, sr, sv: (se[s], 0, jsel(j, sv, s))),
                pl.BlockSpec((None, 1, tf), lambda s, j, se, ss, sr, sv: (se[s], 0, nj + jsel(j, sv, s))),
                pl.BlockSpec((None, 1, D_MODEL), lambda s, j, se, ss, sr, sv: (se[s], 0, 0)),
            ],
            out_specs=pl.BlockSpec(memory_space=pl.ANY),
            scratch_shapes=[
                pltpu.VMEM((cap, D_MODEL), F32),
                pltpu.VMEM((cap, D_MODEL), F32),
                pltpu.SemaphoreType.DMA((cap // unit,)),
                pltpu.SemaphoreType.DMA((cap // unit,)),
            ],
        ),
        out_shape=jax.ShapeDtypeStruct((rows_alloc, D_MODEL), F32),
        compiler_params=pltpu.CompilerParams(
            dimension_semantics=("arbitrary", "arbitrary"), vmem_limit_bytes=VMEM_LIMIT),
        name="experts",
    )(seg_e, seg_start, seg_rows, seg_valid, xs, w_gate_up, w_gate_up, w_down,
      b_gate_up, b_gate_up, b_down)


def _combine_kernel(pos_ref, x2_ref, gate_ref, nf_ref, ys_ref, o_ref, buf, sem, *, tm, t_total):
    i = pl.program_id(0)
    n = pl.num_programs(0)

    def gather(tile, slot):
        def issue(tb, carry):
            for tu in range(ISSUE_UNROLL):
                t = tb * ISSUE_UNROLL + tu
                for k in range(TOP_K):
                    p = pos_ref[k * t_total + tile * tm + t]
                    pltpu.make_async_copy(ys_ref.at[pl.ds(p, 1), :], buf.at[slot, k, pl.ds(t, 1), :],
                                          sem.at[slot]).start()
            return carry

        lax.fori_loop(0, tm // ISSUE_UNROLL, issue, 0)

    @pl.when(i == 0)
    def _first():
        gather(0, 0)

    @pl.when(i + 1 < n)
    def _ahead():
        gather(i + 1, (i + 1) % 2)

    slot = i % 2
    for k in range(TOP_K):
        pltpu.make_async_copy(ys_ref.at[pl.ds(0, tm), :], buf.at[slot, k], sem.at[slot]).wait()

    g = gate_ref[...]
    x = x2_ref[...]
    for k in range(TOP_K):
        x = x + g[:, k:k + 1] * buf[slot, k]
    ms = jnp.mean(x * x, axis=-1, keepdims=True)
    o_ref[...] = x * lax.rsqrt(ms + EPS) * nf_ref[...]


def _combine(pos_flat, x2, gates_t, norm_f_w, ys, *, tm):
    t = x2.shape[0]
    return pl.pallas_call(
        functools.partial(_combine_kernel, tm=tm, t_total=t),
        grid_spec=pltpu.PrefetchScalarGridSpec(
            num_scalar_prefetch=1,
            grid=(t // tm,),
            in_specs=[
                pl.BlockSpec((tm, D_MODEL), lambda i, pos: (i, 0)),
                pl.BlockSpec((tm, TOP_K), lambda i, pos: (i, 0)),
                pl.BlockSpec((1, D_MODEL), lambda i, pos: (0, 0)),
                pl.BlockSpec(memory_space=pl.ANY),
            ],
            out_specs=pl.BlockSpec((tm, D_MODEL), lambda i, pos: (i, 0)),
            scratch_shapes=[pltpu.VMEM((2, TOP_K, tm, D_MODEL), F32), pltpu.SemaphoreType.DMA((2,))],
        ),
        out_shape=jax.ShapeDtypeStruct((t, D_MODEL), F32),
        compiler_params=pltpu.CompilerParams(
            dimension_semantics=("arbitrary",), vmem_limit_bytes=VMEM_LIMIT),
        name="combine",
    )(pos_flat, x2, gates_t, norm_f_w, ys)


def _tiles(t):
    return dict(
        in_tm=min(1024, t), in_tn=512,
        attn_tq=512, attn_tk=256, attn_gw=128, attn_hp=4,
        out_tm=min(512, t), out_sub=256,
        route_tt=min(512, t),
        disp_tm=min(256, t),
        moe_unit=256, moe_tf=512, moe_cap=min(1536, TOP_K * t),
        comb_tm=min(256, t),
    )


def _segments(cnt, offs, *, cap, nseg):
    nseg_e = (cnt + cap - 1) // cap
    ends = jnp.cumsum(nseg_e)
    total = ends[-1]
    s = jnp.arange(nseg, dtype=I32)
    valid = s < total
    s_eff = jnp.minimum(s, jnp.maximum(total - 1, 0))
    e = jnp.minimum(jnp.sum((s_eff[:, None] >= ends[None, :]).astype(I32), axis=1), N_EXPERTS - 1)
    local = s_eff - (ends[e] - nseg_e[e])
    start = offs[e] + local * cap
    rows = jnp.where(valid, jnp.clip(cnt[e] - local * cap, 0, cap), 0)
    return e, start.astype(I32), rows.astype(I32), valid.astype(I32)


def kernel(x, positions, norm1_w, w_in, lambda_q1, lambda_k1, lambda_q2, lambda_k2, attn_subln_w,
           conv_w, conv_b, dt_bias, a_log, d_skip, ssd_norm_w, w_out, norm2_w, router_w, router_b,
           w_gate_up, b_gate_up, w_down, b_down, norm_f_w):
    bsz, seq, _ = x.shape
    t = bsz * seq
    depth = w_in.shape[0]
    assert depth == 1, "single-layer block"
    cfg = _tiles(t)
    layer = 0

    x2d = x.reshape(t, D_MODEL)
    pos2d = positions.reshape(t, 1).astype(I32)
    inv_freq = ROPE_THETA ** (-jnp.arange(0, ROT_DIM, 2, dtype=F32) / ROT_DIM)
    d = jnp.arange(LANES) % ATTN_DH
    invf = jnp.where(d < ROT_DIM, inv_freq[d % ROT_HALF], 0.0).reshape(1, LANES).astype(F32)

    qkv, proj = _in_proj(x2d, pos2d, invf, norm1_w[layer].reshape(1, D_MODEL), w_in[layer].T,
                         tm=cfg["in_tm"], tn=cfg["in_tn"])

    vec = lambda a: a[layer].reshape(1, -1).astype(F32)
    attn = _attention(qkv, vec(lambda_q1), vec(lambda_k1), vec(lambda_q2), vec(lambda_k2),
                      attn_subln_w[layer].astype(F32).reshape(ATTN_DV, 1),
                      bsz=bsz, seq=seq, tq=min(cfg["attn_tq"], seq),
                      tk=min(cfg["attn_tk"], seq), gw=cfg["attn_gw"], hp=cfg["attn_hp"])

    pad_heads = lambda a: jnp.pad(a[layer].astype(F32), (0, LANES - SSD_HEADS)).reshape(1, LANES)
    head_of_lane = jnp.arange(SSD_WIDTH) // SSD_HEADDIM
    expand = (jnp.arange(LANES)[:, None] == head_of_lane[None, :]).astype(BF16)
    dskip_x = d_skip[layer].astype(F32)[head_of_lane].reshape(1, SSD_WIDTH)
    ssd = _ssd(proj, conv_w[layer], conv_b[layer].reshape(1, CONV_DIM), pad_heads(dt_bias),
               pad_heads(a_log), dskip_x, ssd_norm_w[layer].reshape(1, SSD_WIDTH), expand,
               bsz=bsz, seq=seq)

    rw = router_w[layer].astype(F32)
    rw_hi = rw.astype(BF16)
    rw_mid = (rw - rw_hi.astype(F32)).astype(BF16)
    lane_pad = lambda a: jnp.pad(a, ((0, 0), (0, LANES - N_EXPERTS)))
    router_w2 = jnp.concatenate([lane_pad(rw_hi), lane_pad(rw_mid)], axis=1)
    x2, h2, logits_t = _out_proj(
        x2d, attn, ssd, w_out[layer].astype(BF16), norm2_w[layer].reshape(1, D_MODEL),
        router_w2, router_b[layer].reshape(N_EXPERTS, 1),
        tm=cfg["out_tm"], sub=min(cfg["out_sub"], cfg["out_tm"]))

    idx, gates, pos, cnt, offs = _route(logits_t, tt=cfg["route_tt"], row_align=SUBLANES)
    del idx
    pos_flat = pos.reshape(TOP_K * t)

    cap, unit = cfg["moe_cap"], cfg["moe_unit"]
    nseg = N_EXPERTS + (TOP_K * t) // cap
    rows_alloc = TOP_K * t + N_EXPERTS * SUBLANES + unit
    seg_e, seg_start, seg_rows, seg_valid = _segments(
        cnt.reshape(N_EXPERTS), offs.reshape(N_EXPERTS), cap=cap, nseg=nseg)

    xs = _dispatch(pos_flat, h2, tm=cfg["disp_tm"], rows_alloc=rows_alloc)
    ys = _experts(seg_e, seg_start, seg_rows, seg_valid, xs, w_gate_up[layer],
                  b_gate_up[layer].reshape(N_EXPERTS, 1, 2 * D_FF), w_down[layer],
                  b_down[layer].reshape(N_EXPERTS, 1, D_MODEL),
                  unit=unit, tf=cfg["moe_tf"], cap=cap, nseg=nseg)
    out = _combine(pos_flat, x2, gates.T, norm_f_w.reshape(1, D_MODEL), ys, tm=cfg["comb_tm"])
    return out.reshape(bsz, seq, D_MODEL)
```

```python
import functools
import math

import jax
import jax.numpy as jnp
from jax import lax
from jax.experimental import pallas as pl
from jax.experimental.pallas import tpu as pltpu

F32 = jnp.float32
BF16 = jnp.bfloat16
I32 = jnp.int32

D_MODEL = 2048
ATTN_WIDTH = 1024
SSD_WIDTH = 1024
ATTN_HEADS = 8
ATTN_DH = 64
ATTN_DV = 128
ROT_DIM = 16
ROT_HALF = ROT_DIM // 2
ROPE_THETA = 500000.0
SSD_HEADDIM = 64
SSD_HEADS = 16
SSD_GROUPS = 2
SSD_STATE = 128
CONV_W = 4
CONV_DIM = SSD_WIDTH + 2 * SSD_GROUPS * SSD_STATE
CHUNK = 128
N_EXPERTS = 32
TOP_K = 4
D_FF = 2048
SWIGLU_ALPHA = 1.702
SWIGLU_LIMIT = 7.0
EPS = 1e-5
IN_TOTAL = 3 * ATTN_WIDTH + SSD_WIDTH + CONV_DIM + SSD_HEADS
LAMBDA_INIT = 0.8 - 0.6 * math.exp(-0.3 * 0)

LANES = 128
SUBLANES = 8
VMEM_LIMIT = 56 * 1024 * 1024
ISSUE_UNROLL = 8

OFF_Q = 0
OFF_K = ATTN_WIDTH
OFF_V = 2 * ATTN_WIDTH
OFF_Z = 0
OFF_X = OFF_Z + SSD_WIDTH
OFF_B = OFF_X + SSD_WIDTH
OFF_C = OFF_B + SSD_GROUPS * SSD_STATE
OFF_DT = OFF_C + SSD_GROUPS * SSD_STATE


def _sigmoid(x):
    return 1.0 / (1.0 + jnp.exp(-x))


def _softplus(x):
    return jnp.maximum(x, 0.0) + jnp.log1p(jnp.exp(-jnp.abs(x)))


def _in_proj_kernel(pos_ref, invf_ref, x_ref, nw_ref, w_ref, qkv_ref, rest_ref,
                    hn_ref, c_ref, sa_ref, sb_ref, *, tn, n_q_tiles, n_rope_tiles, n_qkv_tiles):
    j = pl.program_id(1)

    @pl.when(j == 0)
    def _prepare():
        x = x_ref[...]
        ms = jnp.mean(x * x, axis=-1, keepdims=True)
        hn_ref[...] = (x * lax.rsqrt(ms + EPS) * nw_ref[...]).astype(BF16)
        ang = pos_ref[...].astype(F32) * invf_ref[...]
        d = lax.broadcasted_iota(I32, ang.shape, 1) % ATTN_DH
        cos = jnp.cos(ang)
        sin = jnp.sin(ang)
        c_ref[...] = jnp.where(d < ROT_DIM, cos, 1.0)
        sa_ref[...] = jnp.where(d < ROT_HALF, 0.0, jnp.where(d < ROT_DIM, sin, 0.0))
        sb_ref[...] = jnp.where(d < ROT_HALF, -sin, 0.0)

    def project():
        return lax.dot_general(hn_ref[...], w_ref[...].astype(BF16), (((1,), (1,)), ((), ())),
                               preferred_element_type=F32)

    @pl.when(j < n_rope_tiles)
    def _rope():
        acc = project()
        reps = tn // LANES
        c = jnp.tile(c_ref[...], (1, reps))
        sa = jnp.tile(sa_ref[...], (1, reps))
        sb = jnp.tile(sb_ref[...], (1, reps))
        r = acc * c + pltpu.roll(acc, ROT_HALF, 1) * sa + pltpu.roll(acc, tn - ROT_HALF, 1) * sb
        scale = jnp.where(j < n_q_tiles, ATTN_DH ** -0.5, 1.0)
        qkv_ref[...] = (r * scale).astype(qkv_ref.dtype)

    @pl.when((j >= n_rope_tiles) & (j < n_qkv_tiles))
    def _value():
        qkv_ref[...] = project().astype(qkv_ref.dtype)

    @pl.when(j >= n_qkv_tiles)
    def _rest():
        rest_ref[...] = project()


def _in_proj(x2d, pos2d, invf, norm_w, w_in_t, *, tm, tn):
    t = x2d.shape[0]
    n = w_in_t.shape[0]
    n_qkv = 3 * ATTN_WIDTH
    nqt = n_qkv // tn
    grid = (t // tm, pl.cdiv(n, tn))
    kern = functools.partial(_in_proj_kernel, tn=tn, n_q_tiles=ATTN_WIDTH // tn,
                             n_rope_tiles=2 * ATTN_WIDTH // tn, n_qkv_tiles=nqt)
    return pl.pallas_call(
        kern,
        grid=grid,
        in_specs=[
            pl.BlockSpec((tm, 1), lambda i, j: (i, 0)),
            pl.BlockSpec((1, LANES), lambda i, j: (0, 0)),
            pl.BlockSpec((tm, D_MODEL), lambda i, j: (i, 0)),
            pl.BlockSpec((1, D_MODEL), lambda i, j: (0, 0)),
            pl.BlockSpec((tn, D_MODEL), lambda i, j: (j, 0)),
        ],
        out_specs=[
            pl.BlockSpec((tm, tn), lambda i, j: (i, jnp.minimum(j, nqt - 1))),
            pl.BlockSpec((tm, tn), lambda i, j: (i, jnp.maximum(j - nqt, 0))),
        ],
        out_shape=[jax.ShapeDtypeStruct((t, n_qkv), BF16),
                   jax.ShapeDtypeStruct((t, n - n_qkv), F32)],
        scratch_shapes=[
            pltpu.VMEM((tm, D_MODEL), BF16),
            pltpu.VMEM((tm, LANES), F32),
            pltpu.VMEM((tm, LANES), F32),
            pltpu.VMEM((tm, LANES), F32),
        ],
        compiler_params=pltpu.CompilerParams(
            dimension_semantics=("parallel", "arbitrary"), vmem_limit_bytes=VMEM_LIMIT),
        name="in_proj",
    )(pos2d, invf, x2d, norm_w, w_in_t)


def _attn_kernel(lq1_ref, lk1_ref, lq2_ref, lk2_ref, q_ref, k_ref, v_ref, w_ref, o_ref, vt_ref,
                 *, tq, tk, gw, hp):
    i = pl.program_id(2)

    @pl.when(i == 0)
    def _transpose_values():
        vt_ref[...] = v_ref[...].astype(F32).T.astype(BF16)

    sub = lax.broadcasted_iota(I32, (ATTN_DV, tq), 0)
    qts = []
    for h in range(hp):
        qt = q_ref[:, h * ATTN_DV:(h + 1) * ATTN_DV].astype(F32).T
        qts.append(jnp.concatenate([jnp.where(sub < ATTN_DH, qt, 0.0),
                                    jnp.where(sub < ATTN_DH, 0.0, qt)], axis=1).astype(BF16))

    ngroups = 2 * tq // gw
    gaps = [lax.broadcasted_iota(I32, (tk, gw), 0)
            - (lax.broadcasted_iota(I32, (tk, gw), 1) + g * gw) % tq for g in range(ngroups)]
    nfull = i * (tq // tk)

    def block(n, stats, diag):
        off = pl.multiple_of(n * tk, tk)
        limit = i * tq - n * tk
        live = [g for g in range(ngroups)
                if diag is None or (g * gw) % tq + gw > diag * tk]
        out = []
        for h in range(hp):
            kb = k_ref[pl.ds(off, tk), h * ATTN_DV:(h + 1) * ATTN_DV]
            vtb = vt_ref[h * ATTN_DV:(h + 1) * ATTN_DV, pl.ds(off, tk)]
            ss = {g: jnp.dot(kb, qts[h][:, g * gw:(g + 1) * gw], preferred_element_type=F32)
                  for g in live}
            mid = {}
            for g in live:
                m, l, _ = stats[h][g]
                s = ss[g]
                if diag is not None and (g * gw) % tq < diag * tk + tk - 1:
                    s = jnp.where(gaps[g] <= limit, s, -jnp.inf)
                m_new = jnp.maximum(m, jnp.max(s, axis=0, keepdims=True))
                a = jnp.exp(m - m_new)
                p = jnp.exp(s - m_new)
                mid[g] = (m_new, a * l + jnp.sum(p, axis=0, keepdims=True), a, p.astype(BF16))
            groups = []
            for g in range(ngroups):
                if g not in mid:
                    groups.append(stats[h][g])
                    continue
                m_new, l, a, p = mid[g]
                acc = a * stats[h][g][2] + jnp.dot(vtb, p, preferred_element_type=F32)
                groups.append((m_new, l, acc))
            out.append(tuple(groups))
        return tuple(out)

    init = tuple(tuple((jnp.full((1, gw), -jnp.inf, F32), jnp.zeros((1, gw), F32),
                        jnp.zeros((ATTN_DV, gw), F32)) for _ in range(ngroups)) for _ in range(hp))
    bpt = tq // tk

    def several(first, stats, masked):
        for r in range(bpt):
            stats = block(first + r, stats, r if masked else None)
        return stats

    stats = lax.fori_loop(0, i, lambda t, c: several(t * bpt, c, False), init)
    stats = several(nfull, stats, True)

    lam = (jnp.exp(jnp.sum(lq1_ref[...] * lk1_ref[...], axis=-1, keepdims=True))
           - jnp.exp(jnp.sum(lq2_ref[...] * lk2_ref[...], axis=-1, keepdims=True)) + LAMBDA_INIT)
    for h in range(hp):
        l = jnp.concatenate([stats[h][g][1] for g in range(ngroups)], axis=1)
        acc = jnp.concatenate([stats[h][g][2] for g in range(ngroups)], axis=1)
        o = acc / l
        d = o[:, :tq] - lam * o[:, tq:]
        ms = jnp.mean(d * d, axis=0, keepdims=True)
        y = (d * lax.rsqrt(ms + EPS) * w_ref[...]) * (1.0 - LAMBDA_INIT)
        o_ref[:, h * ATTN_DV:(h + 1) * ATTN_DV] = y.T.astype(o_ref.dtype)


def _attention(qkv, lq1, lk1, lq2, lk2, subln_w, *, bsz, seq, tq, tk, gw, hp):
    nq = seq // tq
    bw = hp * ATTN_DV
    kb0 = OFF_K // bw
    vb0 = OFF_V // bw
    vec = pl.BlockSpec((1, ATTN_DH), lambda b, h, i: (0, 0))
    return pl.pallas_call(
        functools.partial(_attn_kernel, tq=tq, tk=tk, gw=gw, hp=hp),
        grid=(bsz, ATTN_HEADS // hp, nq),
        in_specs=[
            vec, vec, vec, vec,
            pl.BlockSpec((tq, bw), lambda b, h, i: (b * nq + i, h)),
            pl.BlockSpec((seq, bw), lambda b, h, i: (b, kb0 + h)),
            pl.BlockSpec((seq, bw), lambda b, h, i: (b, vb0 + h)),
            pl.BlockSpec((ATTN_DV, 1), lambda b, h, i: (0, 0)),
        ],
        out_specs=pl.BlockSpec((tq, bw), lambda b, h, i: (b * nq + i, h)),
        out_shape=jax.ShapeDtypeStruct((bsz * seq, ATTN_WIDTH), BF16),
        scratch_shapes=[pltpu.VMEM((bw, seq), BF16)],
        compiler_params=pltpu.CompilerParams(
            dimension_semantics=("parallel", "parallel", "arbitrary"), vmem_limit_bytes=VMEM_LIMIT),
        name="diff_attention",
    )(lq1, lk1, lq2, lk2, qkv, qkv, qkv, subln_w)


def _ssd_kernel(z_ref, xs_ref, b_ref, c_ref, dt_ref, cw_ref, cb_ref, dtb_ref, alog_ref,
                dskip_ref, nw_ref, e_ref, o_ref, ext_ref, state_ref):
    ci = pl.program_id(1)
    q = CHUNK
    gw = SSD_WIDTH // SSD_GROUPS

    @pl.when(ci == 0)
    def _init():
        ext_ref[0:SUBLANES, :] = jnp.zeros((SUBLANES, CONV_DIM), F32)
        state_ref[...] = jnp.zeros_like(state_ref)

    u = jnp.concatenate([xs_ref[...], b_ref[...], c_ref[...]], axis=1)
    tail = ext_ref[0:SUBLANES, :]
    row8 = lax.broadcasted_iota(I32, (SUBLANES, CONV_DIM), 0)
    acc = cb_ref[...] + u * cw_ref[CONV_W - 1:CONV_W, :]
    for w in range(CONV_W - 1):
        sh = CONV_W - 1 - w
        rolled = pltpu.roll(u, sh, 0)
        head = jnp.where(row8 < sh, pltpu.roll(tail, sh, 0), rolled[0:SUBLANES, :])
        acc = acc + jnp.concatenate([head, rolled[SUBLANES:, :]], axis=0) * cw_ref[w:w + 1, :]
    ext_ref[0:SUBLANES, :] = u[q - SUBLANES:q, :]
    xbc = acc * _sigmoid(acc)
    xc = xbc[:, :SSD_WIDTH]
    bm = xbc[:, SSD_WIDTH:SSD_WIDTH + SSD_GROUPS * SSD_STATE]
    cm = xbc[:, SSD_WIDTH + SSD_GROUPS * SSD_STATE:]

    lane = lax.broadcasted_iota(I32, (q, LANES), 1)
    head_ok = lane < SSD_HEADS
    dt = jnp.where(head_ok, _softplus(jnp.where(head_ok, dt_ref[...], 0.0) + dtb_ref[...]), 0.0)
    da = dt * (-jnp.exp(alog_ref[...]))
    row = lax.broadcasted_iota(I32, (q, q), 0)
    col = lax.broadcasted_iota(I32, (q, q), 1)
    causal = row >= col
    cs = jnp.dot(causal.astype(F32), da, preferred_element_type=F32,
                 precision=lax.Precision.HIGHEST)
    cs_t = cs.T
    cs_last = cs[q - 1:q, :]

    stk = jnp.concatenate([dt, jnp.exp(cs), jnp.exp(cs_last - cs)], axis=0)
    hi = stk.astype(BF16)
    lo = (stk - hi.astype(F32)).astype(BF16)
    ex = (jnp.dot(hi, e_ref[...], preferred_element_type=F32)
          + jnp.dot(lo, e_ref[...], preferred_element_type=F32))
    dtx = ex[0:q]
    ecx = ex[q:2 * q]
    dsx = ex[2 * q:3 * q]

    xdt = xc * dtx
    xdt_b = xdt.astype(BF16)
    xw_b = (xdt * dsx).astype(BF16)
    lane_q = lax.broadcasted_iota(I32, (q, LANES), 1)

    y_groups = []
    for g in range(SSD_GROUPS):
        gs = slice(g * gw, (g + 1) * gw)
        bg = bm[:, g * SSD_STATE:(g + 1) * SSD_STATE]
        cg_b = cm[:, g * SSD_STATE:(g + 1) * SSD_STATE].astype(BF16)
        cb = lax.dot_general(cg_b, bg.astype(BF16), (((1,), (1,)), ((), ())),
                             preferred_element_type=F32)
        prev_t = state_ref[g]
        y_off = jnp.dot(cg_b, prev_t.astype(BF16), preferred_element_type=F32) * ecx[:, gs]
        st_t = jnp.dot(bg.T.astype(BF16), xw_b[:, gs], preferred_element_type=F32)
        state_ref[g] = prev_t * ecx[q - 1:q, gs] + st_t

        pairs = []
        for jp in range(gw // LANES):
            rhs = xdt_b[:, g * gw + jp * LANES: g * gw + (jp + 1) * LANES]
            outs = []
            for hh in range(2):
                h = g * (SSD_HEADS // SSD_GROUPS) + jp * 2 + hh
                seg = (jnp.broadcast_to(cs[:, h:h + 1], (q, q))
                       - jnp.broadcast_to(cs_t[h:h + 1, :], (q, q)))
                dec = jnp.exp(jnp.where(causal, seg, -jnp.inf))
                outs.append(jnp.dot((cb * dec).astype(BF16), rhs, preferred_element_type=F32))
            pairs.append(jnp.where(lane_q < SSD_HEADDIM, outs[0], outs[1]))
        y_groups.append(jnp.concatenate(pairs, axis=1) + y_off)

    y = jnp.concatenate(y_groups, axis=1) + dskip_ref[...] * xc
    z = z_ref[...]
    y = y * (z * _sigmoid(z))
    outs = []
    for g in range(SSD_GROUPS):
        yg = y[:, g * gw:(g + 1) * gw]
        outs.append(yg * lax.rsqrt(jnp.mean(yg * yg, axis=-1, keepdims=True) + EPS))
    o_ref[...] = (jnp.concatenate(outs, axis=1) * nw_ref[...]).astype(o_ref.dtype)


def _ssd(proj, conv_w, conv_b, dt_bias_p, a_log_p, dskip_x, ssd_norm_w, expand, *, bsz, seq):
    nc = seq // CHUNK
    sw = SSD_GROUPS * SSD_STATE
    const = lambda shape: pl.BlockSpec(shape, lambda b, c: (0, 0))
    return pl.pallas_call(
        _ssd_kernel,
        grid=(bsz, nc),
        in_specs=[
            pl.BlockSpec((CHUNK, SSD_WIDTH), lambda b, c: (b * nc + c, OFF_Z // SSD_WIDTH)),
            pl.BlockSpec((CHUNK, SSD_WIDTH), lambda b, c: (b * nc + c, OFF_X // SSD_WIDTH)),
            pl.BlockSpec((CHUNK, sw), lambda b, c: (b * nc + c, OFF_B // sw)),
            pl.BlockSpec((CHUNK, sw), lambda b, c: (b * nc + c, OFF_C // sw)),
            pl.BlockSpec((CHUNK, LANES), lambda b, c: (b * nc + c, OFF_DT // LANES)),
            const((CONV_W, CONV_DIM)),
            const((1, CONV_DIM)),
            const((1, LANES)),
            const((1, LANES)),
            const((1, SSD_WIDTH)),
            const((1, SSD_WIDTH)),
            const((LANES, SSD_WIDTH)),
        ],
        out_specs=pl.BlockSpec((CHUNK, SSD_WIDTH), lambda b, c: (b * nc + c, 0)),
        out_shape=jax.ShapeDtypeStruct((bsz * seq, SSD_WIDTH), BF16),
        scratch_shapes=[
            pltpu.VMEM((SUBLANES, CONV_DIM), F32),
            pltpu.VMEM((SSD_GROUPS, SSD_STATE, SSD_WIDTH // SSD_GROUPS), F32),
        ],
        compiler_params=pltpu.CompilerParams(
            dimension_semantics=("parallel", "arbitrary"), vmem_limit_bytes=VMEM_LIMIT),
        name="ssd",
    )(proj, proj, proj, proj, proj, conv_w, conv_b, dt_bias_p, a_log_p, dskip_x, ssd_norm_w, expand)


def _out_proj_kernel(x_ref, a_ref, s_ref, wo_ref, n2_ref, rw_ref, rb_ref, x2_ref, h2_ref, lg_ref,
                     *, sub):
    for r0 in range(0, x_ref.shape[0], sub):
        rs = slice(r0, r0 + sub)
        cat = jnp.concatenate([a_ref[rs, :], s_ref[rs, :]], axis=1)
        x2 = x_ref[rs, :] + jnp.dot(cat, wo_ref[...], preferred_element_type=F32)
        x2_ref[rs, :] = x2
        ms = jnp.mean(x2 * x2, axis=-1, keepdims=True)
        h2 = x2 * lax.rsqrt(ms + EPS) * n2_ref[...]
        h2_ref[rs, :] = h2
        h_hi = h2.astype(BF16)
        h_mid = (h2 - h_hi.astype(F32)).astype(BF16)
        p1 = jnp.dot(h_hi, rw_ref[...], preferred_element_type=F32)
        p2 = jnp.dot(h_mid, rw_ref[:, 0:LANES], preferred_element_type=F32)
        lg = p1[:, 0:LANES] + p1[:, LANES:] + p2
        lg_ref[:, rs] = lg.T[0:N_EXPERTS, :] + rb_ref[...]


def _out_proj(x2d, attn, ssd, w_out_b, norm2_w, router_w2, router_b, *, tm, sub):
    t = x2d.shape[0]
    const = lambda shape: pl.BlockSpec(shape, lambda i: (0, 0))
    return pl.pallas_call(
        functools.partial(_out_proj_kernel, sub=sub),
        grid=(t // tm,),
        in_specs=[
            pl.BlockSpec((tm, D_MODEL), lambda i: (i, 0)),
            pl.BlockSpec((tm, ATTN_WIDTH), lambda i: (i, 0)),
            pl.BlockSpec((tm, SSD_WIDTH), lambda i: (i, 0)),
            const((D_MODEL, D_MODEL)),
            const((1, D_MODEL)),
            const((D_MODEL, 2 * LANES)),
            const((N_EXPERTS, 1)),
        ],
        out_specs=[
            pl.BlockSpec((tm, D_MODEL), lambda i: (i, 0)),
            pl.BlockSpec((tm, D_MODEL), lambda i: (i, 0)),
            pl.BlockSpec((N_EXPERTS, tm), lambda i: (0, i)),
        ],
        out_shape=[
            jax.ShapeDtypeStruct((t, D_MODEL), F32),
            jax.ShapeDtypeStruct((t, D_MODEL), F32),
            jax.ShapeDtypeStruct((N_EXPERTS, t), F32),
        ],
        compiler_params=pltpu.CompilerParams(
            dimension_semantics=("parallel",), vmem_limit_bytes=VMEM_LIMIT),
        name="out_proj",
    )(x2d, attn, ssd, w_out_b, norm2_w, router_w2, router_b)


def _route_kernel(lg_ref, idx_ref, gate_ref, pos_ref, cnt_ref, off_ref, rank_ref, *, tt, row_align):
    t_total = lg_ref.shape[1]
    nt = t_total // tt
    eio = lax.broadcasted_iota(I32, (N_EXPERTS, tt), 0)
    r = lax.broadcasted_iota(I32, (tt, tt), 0)
    c = lax.broadcasted_iota(I32, (tt, tt), 1)
    before = jnp.where(r < c, 1.0, 0.0).astype(BF16)

    def tile(ti, cnt):
        off = pl.multiple_of(ti * tt, tt)
        work = lg_ref[:, pl.ds(off, tt)]
        vals, hots = [], []
        for k in range(TOP_K):
            m = jnp.max(work, axis=0, keepdims=True)
            ik = jnp.min(jnp.where(work == m, eio, N_EXPERTS), axis=0, keepdims=True)
            hot = eio == ik
            work = jnp.where(hot, -jnp.inf, work)
            idx_ref[k:k + 1, pl.ds(off, tt)] = ik
            vals.append(m)
            hots.append(hot)
        es = [jnp.exp(v - vals[0]) for v in vals]
        inv = 1.0 / (es[0] + es[1] + es[2] + es[3])
        sel = jnp.zeros((N_EXPERTS, tt), F32)
        for k in range(TOP_K):
            gate_ref[k:k + 1, pl.ds(off, tt)] = es[k] * inv
            sel = sel + jnp.where(hots[k], 1.0, 0.0)
        rank = jnp.dot(sel.astype(BF16), before, preferred_element_type=F32) + cnt
        for k in range(TOP_K):
            rank_ref[k:k + 1, pl.ds(off, tt)] = jnp.sum(jnp.where(hots[k], rank, 0.0), axis=0, keepdims=True)
        return cnt + jnp.sum(sel, axis=1, keepdims=True)

    cnt = lax.fori_loop(0, nt, tile, jnp.zeros((N_EXPERTS, 1), F32))
    cnt_ref[...] = cnt.astype(I32)
    padded = jnp.ceil(cnt * (1.0 / row_align)) * row_align
    er = lax.broadcasted_iota(I32, (N_EXPERTS, N_EXPERTS), 0)
    ec = lax.broadcasted_iota(I32, (N_EXPERTS, N_EXPERTS), 1)
    lower = jnp.where(ec < er, 1.0, 0.0)
    offs = jnp.dot(lower, jnp.broadcast_to(padded, (N_EXPERTS, LANES)), preferred_element_type=F32,
                   precision=lax.Precision.HIGHEST)[:, 0:1]
    off_ref[...] = offs.astype(I32)

    def place(ti, carry):
        off = pl.multiple_of(ti * tt, tt)
        for k in range(TOP_K):
            hot = eio == idx_ref[k:k + 1, pl.ds(off, tt)]
            base = jnp.sum(jnp.where(hot, offs, 0.0), axis=0, keepdims=True)
            pos_ref[k:k + 1, pl.ds(off, tt)] = (base + rank_ref[k:k + 1, pl.ds(off, tt)]).astype(I32)
        return carry

    lax.fori_loop(0, nt, place, 0)


def _route(logits_t, *, tt, row_align):
    t = logits_t.shape[1]
    full = lambda shape: pl.BlockSpec(shape, lambda i: (0, 0))
    return pl.pallas_call(
        functools.partial(_route_kernel, tt=tt, row_align=row_align),
        grid=(1,),
        in_specs=[full((N_EXPERTS, t))],
        out_specs=[full((TOP_K, t)), full((TOP_K, t)), full((TOP_K, t)),
                   full((N_EXPERTS, 1)), full((N_EXPERTS, 1))],
        out_shape=[
            jax.ShapeDtypeStruct((TOP_K, t), I32),
            jax.ShapeDtypeStruct((TOP_K, t), F32),
            jax.ShapeDtypeStruct((TOP_K, t), I32),
            jax.ShapeDtypeStruct((N_EXPERTS, 1), I32),
            jax.ShapeDtypeStruct((N_EXPERTS, 1), I32),
        ],
        scratch_shapes=[pltpu.VMEM((TOP_K, t), F32)],
        compiler_params=pltpu.CompilerParams(
            dimension_semantics=("arbitrary",), vmem_limit_bytes=VMEM_LIMIT),
        name="route",
    )(logits_t)


def _dispatch_kernel(pos_ref, h_ref, xs_ref, sem, *, tm, t_total):
    i = pl.program_id(0)

    def issue(tb, carry):
        for tu in range(ISSUE_UNROLL):
            t = pl.multiple_of(tb * ISSUE_UNROLL, ISSUE_UNROLL) + tu
            for k in range(TOP_K):
                p = pos_ref[k * t_total + i * tm + t]
                pltpu.make_async_copy(h_ref.at[pl.ds(t, 1), :], xs_ref.at[pl.ds(p, 1), :], sem).start()
        return carry

    lax.fori_loop(0, tm // ISSUE_UNROLL, issue, 0)
    for k in range(TOP_K):
        pltpu.make_async_copy(h_ref, xs_ref.at[pl.ds(0, tm), :], sem).wait()


def _dispatch(pos_flat, h2, *, tm, rows_alloc):
    t = h2.shape[0]
    return pl.pallas_call(
        functools.partial(_dispatch_kernel, tm=tm, t_total=t),
        grid_spec=pltpu.PrefetchScalarGridSpec(
            num_scalar_prefetch=1,
            grid=(t // tm,),
            in_specs=[pl.BlockSpec((tm, D_MODEL), lambda i, pos: (i, 0))],
            out_specs=pl.BlockSpec(memory_space=pl.ANY),
            scratch_shapes=[pltpu.SemaphoreType.DMA(())],
        ),
        out_shape=jax.ShapeDtypeStruct((rows_alloc, D_MODEL), F32),
        compiler_params=pltpu.CompilerParams(
            dimension_semantics=("arbitrary",), vmem_limit_bytes=VMEM_LIMIT),
        name="dispatch",
    )(pos_flat, h2)


def _experts_kernel(se_ref, ss_ref, sr_ref, sv_ref, xs_ref, wg_ref, wu_ref, wd_ref, bg_ref, bu_ref,
                    bd_ref, ys_ref, xseg, yacc, sem_in, sem_out, *, unit, max_units, nj):
    s = pl.program_id(0)
    j = pl.program_id(1)
    nseg = pl.num_programs(0)
    rows = sr_ref[s]
    start = ss_ref[s]
    sliver = unit // 4
    ahead = 4

    def plan(r):
        nu = (r + unit - 1) // unit
        return nu, (nu % 2 == 1) & (nu >= 3) & (r - (nu - 1) * unit <= sliver)

    nunits, fuse = plan(rows)
    odd = nunits % 2 == 1
    nbig = nunits // 2 - fuse.astype(I32)
    prev_units, prev_fuse = plan(sr_ref[jnp.maximum(s - 1, 0)])
    prev_units = jnp.where(s > 0, prev_units, 0)
    nxt = jnp.minimum(s + 1, nseg - 1)
    has_next = (s + 1 < nseg) & (sr_ref[nxt] > 0)
    next_units = jnp.where(has_next, plan(sr_ref[nxt])[0], 0)

    def x_copy(u, base):
        r0 = pl.multiple_of(base + u * unit, SUBLANES)
        l0 = pl.multiple_of(u * unit, unit)
        return pltpu.make_async_copy(xs_ref.at[pl.ds(r0, unit), :], xseg.at[pl.ds(l0, unit), :],
                                     sem_in.at[u])

    def y_copy(u, nrows=unit):
        r0 = pl.multiple_of(start + u * unit, SUBLANES)
        l0 = pl.multiple_of(u * unit, unit)
        return pltpu.make_async_copy(yacc.at[pl.ds(l0, nrows), :], ys_ref.at[pl.ds(r0, nrows), :],
                                     sem_out.at[u])

    def fetch(u):
        @pl.when((u < nunits) & (u >= prev_units))
        def _():
            x_copy(u, start).start()

    def await_copy_out(u, n_units, fused):
        @pl.when((u < n_units) & jnp.logical_not(fused & (u == n_units - 1)))
        def _whole():
            y_copy(u).wait()

        @pl.when((u < n_units) & fused & (u == n_units - 1))
        def _part():
            y_copy(u, sliver).wait()

    def chunk(u0, m):
        nu = -(-m // unit)
        l0 = pl.multiple_of(u0 * unit, unit)

        @pl.when(j == 0)
        def _arrived():
            if m == 2 * unit:
                for k in range(2):
                    fetch(u0 + ahead + k)
            for k in range(nu):
                x_copy(u0 + k, start).wait()
                await_copy_out(u0 + k, prev_units, prev_fuse)

        x = xseg[pl.ds(l0, m), :].astype(BF16)
        g = jnp.dot(x, wg_ref[...].astype(BF16), preferred_element_type=F32) + bg_ref[...]
        u = jnp.dot(x, wu_ref[...].astype(BF16), preferred_element_type=F32) + bu_ref[...]
        g = jnp.minimum(g, SWIGLU_LIMIT)
        u = jnp.clip(u, -SWIGLU_LIMIT, SWIGLU_LIMIT)
        act = (g * _sigmoid(SWIGLU_ALPHA * g) * (u + 1.0)).astype(BF16)

        @pl.when(j == 0)
        def _first():
            yacc[pl.ds(l0, m), :] = bd_ref[...] + jnp.dot(
                act, wd_ref[...].astype(BF16), preferred_element_type=F32)

        @pl.when(j > 0)
        def _rest():
            yacc[pl.ds(l0, m), :] = yacc[pl.ds(l0, m), :] + jnp.dot(
                act, wd_ref[...].astype(BF16), preferred_element_type=F32)

        @pl.when(j == nj - 1)
        def _flush():
            for k in range(m // unit):
                y_copy(u0 + k).start()
            if m % unit:
                y_copy(u0 + m // unit, m % unit).start()
            for k in range(nu):
                @pl.when(u0 + k < next_units)
                def _():
                    x_copy(u0 + k, ss_ref[nxt]).start()

    @pl.when(rows > 0)
    def _work():
        @pl.when(j == 0)
        def _begin():
            for u in range(max_units):
                @pl.when(u >= nunits)
                def _():
                    await_copy_out(u, prev_units, prev_fuse)

            for u in range(ahead):
                fetch(u)

        def big(c, carry):
            chunk(2 * c, 2 * unit)
            return carry

        lax.fori_loop(0, nbig, big, 0)

        @pl.when(fuse)
        def _with_sliver():
            chunk(2 * nbig, 2 * unit + sliver)

        @pl.when(odd & jnp.logical_not(fuse))
        def _tail():
            chunk(2 * nbig, unit)

        @pl.when((j == nj - 1) & jnp.logical_not(has_next))
        def _final():
            for u in range(max_units):
                await_copy_out(u, nunits, fuse)


def _experts(seg_e, seg_start, seg_rows, seg_valid, xs, w_gate_up, b_gate_up, w_down, b_down,
             *, unit, tf, cap, nseg):
    nj = D_FF // tf
    rows_alloc = xs.shape[0]

    def jsel(j, sv, s):
        return j * sv[s] + (nj - 1) * (1 - sv[s])

    return pl.pallas_call(
        functools.partial(_experts_kernel, unit=unit, max_units=cap // unit, nj=nj),
        grid_spec=pltpu.PrefetchScalarGridSpec(
            num_scalar_prefetch=4,
            grid=(nseg, nj),
            in_specs=[
                pl.BlockSpec(memory_space=pl.ANY),
                pl.BlockSpec((None, D_MODEL, tf), lambda s, j, se, ss, sr, sv: (se[s], 0, jsel(j, sv, s))),
                pl.BlockSpec((None, D_MODEL, tf), lambda s, j, se, ss, sr, sv: (se[s], 0, nj + jsel(j, sv, s))),
                pl.BlockSpec((None, tf, D_MODEL), lambda s, j, se, ss, sr, sv: (se[s], jsel(j, sv, s), 0)),
                pl.BlockSpec((None, 1, tf), lambda s, j, se, ss, sr, sv: (se[s], 0, jsel(j, sv, s))),
                pl.BlockSpec((None, 1, tf), lambda s, j, se, ss, sr, sv: (se[s], 0, nj + jsel(j, sv, s))),
                pl.BlockSpec((None, 1, D_MODEL), lambda s, j, se, ss, sr, sv: (se[s], 0, 0)),
            ],
            out_specs=pl.BlockSpec(memory_space=pl.ANY),
            scratch_shapes=[
                pltpu.VMEM((cap, D_MODEL), F32),
                pltpu.VMEM((cap, D_MODEL), F32),
                pltpu.SemaphoreType.DMA((cap // unit,)),
                pltpu.SemaphoreType.DMA((cap // unit,)),
            ],
        ),
        out_shape=jax.ShapeDtypeStruct((rows_alloc, D_MODEL), F32),
        compiler_params=pltpu.CompilerParams(
            dimension_semantics=("arbitrary", "arbitrary"), vmem_limit_bytes=VMEM_LIMIT),
        name="experts",
    )(seg_e, seg_start, seg_rows, seg_valid, xs, w_gate_up, w_gate_up, w_down,
      b_gate_up, b_gate_up, b_down)


def _combine_kernel(pos_ref, x2_ref, gate_ref, nf_ref, ys_ref, o_ref, buf, sem, *, tm, t_total):
    i = pl.program_id(0)
    n = pl.num_programs(0)

    def gather(tile, slot):
        def issue(tb, carry):
            for tu in range(ISSUE_UNROLL):
                t = pl.multiple_of(tb * ISSUE_UNROLL, ISSUE_UNROLL) + tu
                for k in range(TOP_K):
                    p = pos_ref[k * t_total + tile * tm + t]
                    pltpu.make_async_copy(ys_ref.at[pl.ds(p, 1), :], buf.at[slot, k, pl.ds(t, 1), :],
                                          sem.at[slot]).start()
            return carry

        lax.fori_loop(0, tm // ISSUE_UNROLL, issue, 0)

    @pl.when(i == 0)
    def _first():
        gather(0, 0)

    @pl.when(i + 1 < n)
    def _ahead():
        gather(i + 1, (i + 1) % 2)

    slot = i % 2
    for k in range(TOP_K):
        pltpu.make_async_copy(ys_ref.at[pl.ds(0, tm), :], buf.at[slot, k], sem.at[slot]).wait()

    g = gate_ref[...]
    x = x2_ref[...]
    for k in range(TOP_K):
        x = x + g[:, k:k + 1] * buf[slot, k]
    ms = jnp.mean(x * x, axis=-1, keepdims=True)
    o_ref[...] = x * lax.rsqrt(ms + EPS) * nf_ref[...]


def _combine(pos_flat, x2, gates_t, norm_f_w, ys, *, tm):
    t = x2.shape[0]
    return pl.pallas_call(
        functools.partial(_combine_kernel, tm=tm, t_total=t),
        grid_spec=pltpu.PrefetchScalarGridSpec(
            num_scalar_prefetch=1,
            grid=(t // tm,),
            in_specs=[
                pl.BlockSpec((tm, D_MODEL), lambda i, pos: (i, 0)),
                pl.BlockSpec((tm, TOP_K), lambda i, pos: (i, 0)),
                pl.BlockSpec((1, D_MODEL), lambda i, pos: (0, 0)),
                pl.BlockSpec(memory_space=pl.ANY),
            ],
            out_specs=pl.BlockSpec((tm, D_MODEL), lambda i, pos: (i, 0)),
            scratch_shapes=[pltpu.VMEM((2, TOP_K, tm, D_MODEL), F32), pltpu.SemaphoreType.DMA((2,))],
        ),
        out_shape=jax.ShapeDtypeStruct((t, D_MODEL), F32),
        compiler_params=pltpu.CompilerParams(
            dimension_semantics=("arbitrary",), vmem_limit_bytes=VMEM_LIMIT),
        name="combine",
    )(pos_flat, x2, gates_t, norm_f_w, ys)


def _tiles(t):
    return dict(
        in_tm=min(1024, t), in_tn=512,
        attn_tq=512, attn_tk=256, attn_gw=128, attn_hp=4,
        out_tm=min(512, t), out_sub=256,
        route_tt=min(512, t),
        disp_tm=min(256, t),
        moe_unit=256, moe_tf=512, moe_cap=min(1536, TOP_K * t),
        comb_tm=min(256, t),
    )


def _segments(cnt, offs, *, cap, nseg):
    nseg_e = (cnt + cap - 1) // cap
    ends = jnp.cumsum(nseg_e)
    total = ends[-1]
    s = jnp.arange(nseg, dtype=I32)
    valid = s < total
    s_eff = jnp.minimum(s, jnp.maximum(total - 1, 0))
    e = jnp.minimum(jnp.sum((s_eff[:, None] >= ends[None, :]).astype(I32), axis=1), N_EXPERTS - 1)
    local = s_eff - (ends[e] - nseg_e[e])
    start = offs[e] + local * cap
    rows = jnp.where(valid, jnp.clip(cnt[e] - local * cap, 0, cap), 0)
    return e, start.astype(I32), rows.astype(I32), valid.astype(I32)


def kernel(x, positions, norm1_w, w_in, lambda_q1, lambda_k1, lambda_q2, lambda_k2, attn_subln_w,
           conv_w, conv_b, dt_bias, a_log, d_skip, ssd_norm_w, w_out, norm2_w, router_w, router_b,
           w_gate_up, b_gate_up, w_down, b_down, norm_f_w):
    bsz, seq, _ = x.shape
    t = bsz * seq
    depth = w_in.shape[0]
    assert depth == 1, "single-layer block"
    cfg = _tiles(t)
    layer = 0

    x2d = x.reshape(t, D_MODEL)
    pos2d = positions.reshape(t, 1).astype(I32)
    inv_freq = ROPE_THETA ** (-jnp.arange(0, ROT_DIM, 2, dtype=F32) / ROT_DIM)
    d = jnp.arange(LANES) % ATTN_DH
    invf = jnp.where(d < ROT_DIM, inv_freq[d % ROT_HALF], 0.0).reshape(1, LANES).astype(F32)

    qkv, proj = _in_proj(x2d, pos2d, invf, norm1_w[layer].reshape(1, D_MODEL), w_in[layer].T,
                         tm=cfg["in_tm"], tn=cfg["in_tn"])

    vec = lambda a: a[layer].reshape(1, -1).astype(F32)
    attn = _attention(qkv, vec(lambda_q1), vec(lambda_k1), vec(lambda_q2), vec(lambda_k2),
                      attn_subln_w[layer].astype(F32).reshape(ATTN_DV, 1),
                      bsz=bsz, seq=seq, tq=min(cfg["attn_tq"], seq),
                      tk=min(cfg["attn_tk"], seq), gw=cfg["attn_gw"], hp=cfg["attn_hp"])

    pad_heads = lambda a: jnp.pad(a[layer].astype(F32), (0, LANES - SSD_HEADS)).reshape(1, LANES)
    head_of_lane = jnp.arange(SSD_WIDTH) // SSD_HEADDIM
    expand = (jnp.arange(LANES)[:, None] == head_of_lane[None, :]).astype(BF16)
    dskip_x = jnp.repeat(d_skip[layer].astype(F32), SSD_HEADDIM).reshape(1, SSD_WIDTH)
    ssd = _ssd(proj, conv_w[layer], conv_b[layer].reshape(1, CONV_DIM), pad_heads(dt_bias),
               pad_heads(a_log), dskip_x, ssd_norm_w[layer].reshape(1, SSD_WIDTH), expand,
               bsz=bsz, seq=seq)

    rw = router_w[layer].astype(F32)
    rw_hi = rw.astype(BF16)
    rw_mid = (rw - rw_hi.astype(F32)).astype(BF16)
    lane_pad = lambda a: jnp.pad(a, ((0, 0), (0, LANES - N_EXPERTS)))
    router_w2 = jnp.concatenate([lane_pad(rw_hi), lane_pad(rw_mid)], axis=1)
    x2, h2, logits_t = _out_proj(
        x2d, attn, ssd, w_out[layer].astype(BF16), norm2_w[layer].reshape(1, D_MODEL),
        router_w2, router_b[layer].reshape(N_EXPERTS, 1),
        tm=cfg["out_tm"], sub=min(cfg["out_sub"], cfg["out_tm"]))

    idx, gates, pos, cnt, offs = _route(logits_t, tt=cfg["route_tt"], row_align=SUBLANES)
    del idx
    pos_flat = pos.reshape(TOP_K * t)

    cap, unit = cfg["moe_cap"], cfg["moe_unit"]
    nseg = N_EXPERTS + (TOP_K * t) // cap
    rows_alloc = TOP_K * t + N_EXPERTS * SUBLANES + unit
    seg_e, seg_start, seg_rows, seg_valid = _segments(
        cnt.reshape(N_EXPERTS), offs.reshape(N_EXPERTS), cap=cap, nseg=nseg)

    xs = _dispatch(pos_flat, h2, tm=cfg["disp_tm"], rows_alloc=rows_alloc)
    ys = _experts(seg_e, seg_start, seg_rows, seg_valid, xs, w_gate_up[layer],
                  b_gate_up[layer].reshape(N_EXPERTS, 1, 2 * D_FF), w_down[layer],
                  b_down[layer].reshape(N_EXPERTS, 1, D_MODEL),
                  unit=unit, tf=cfg["moe_tf"], cap=cap, nseg=nseg)
    out = _combine(pos_flat, x2, gates.T, norm_f_w.reshape(1, D_MODEL), ys, tm=cfg["comb_tm"])
    return out.reshape(bsz, seq, D_MODEL)
```

```python
import functools
import math

import jax
import jax.numpy as jnp
from jax import lax
from jax.experimental import pallas as pl
from jax.experimental.pallas import tpu as pltpu

F32 = jnp.float32
BF16 = jnp.bfloat16
I32 = jnp.int32

D_MODEL = 2048
ATTN_WIDTH = 1024
SSD_WIDTH = 1024
ATTN_HEADS = 8
ATTN_DH = 64
ATTN_DV = 128
ROT_DIM = 16
ROT_HALF = ROT_DIM // 2
ROPE_THETA = 500000.0
SSD_HEADDIM = 64
SSD_HEADS = 16
SSD_GROUPS = 2
SSD_STATE = 128
CONV_W = 4
CONV_DIM = SSD_WIDTH + 2 * SSD_GROUPS * SSD_STATE
CHUNK = 128
N_EXPERTS = 32
TOP_K = 4
D_FF = 2048
SWIGLU_ALPHA = 1.702
SWIGLU_LIMIT = 7.0
EPS = 1e-5
IN_TOTAL = 3 * ATTN_WIDTH + SSD_WIDTH + CONV_DIM + SSD_HEADS
LAMBDA_INIT = 0.8 - 0.6 * math.exp(-0.3 * 0)

LANES = 128
SUBLANES = 8
VMEM_LIMIT = 56 * 1024 * 1024
ISSUE_UNROLL = 8

OFF_Q = 0
OFF_K = ATTN_WIDTH
OFF_V = 2 * ATTN_WIDTH
OFF_Z = 0
OFF_X = OFF_Z + SSD_WIDTH
OFF_B = OFF_X + SSD_WIDTH
OFF_C = OFF_B + SSD_GROUPS * SSD_STATE
OFF_DT = OFF_C + SSD_GROUPS * SSD_STATE


def _sigmoid(x):
    return 1.0 / (1.0 + jnp.exp(-x))


def _softplus(x):
    return jnp.maximum(x, 0.0) + jnp.log1p(jnp.exp(-jnp.abs(x)))


def _in_proj_kernel(pos_ref, invf_ref, x_ref, nw_ref, w_ref, qkv_ref, rest_ref,
                    hn_ref, c_ref, sa_ref, sb_ref, *, tn, n_q_tiles, n_rope_tiles, n_qkv_tiles):
    j = pl.program_id(1)

    @pl.when(j == 0)
    def _prepare():
        x = x_ref[...]
        ms = jnp.mean(x * x, axis=-1, keepdims=True)
        hn_ref[...] = (x * lax.rsqrt(ms + EPS) * nw_ref[...]).astype(BF16)
        ang = pos_ref[...].astype(F32) * invf_ref[...]
        d = lax.broadcasted_iota(I32, ang.shape, 1) % ATTN_DH
        cos = jnp.cos(ang)
        sin = jnp.sin(ang)
        c_ref[...] = jnp.where(d < ROT_DIM, cos, 1.0)
        sa_ref[...] = jnp.where(d < ROT_HALF, 0.0, jnp.where(d < ROT_DIM, sin, 0.0))
        sb_ref[...] = jnp.where(d < ROT_HALF, -sin, 0.0)

    def project():
        return lax.dot_general(hn_ref[...], w_ref[...].astype(BF16), (((1,), (1,)), ((), ())),
                               preferred_element_type=F32)

    @pl.when(j < n_rope_tiles)
    def _rope():
        acc = project()
        reps = tn // LANES
        c = jnp.tile(c_ref[...], (1, reps))
        sa = jnp.tile(sa_ref[...], (1, reps))
        sb = jnp.tile(sb_ref[...], (1, reps))
        r = acc * c + pltpu.roll(acc, ROT_HALF, 1) * sa + pltpu.roll(acc, tn - ROT_HALF, 1) * sb
        scale = jnp.where(j < n_q_tiles, ATTN_DH ** -0.5, 1.0)
        qkv_ref[...] = (r * scale).astype(qkv_ref.dtype)

    @pl.when((j >= n_rope_tiles) & (j < n_qkv_tiles))
    def _value():
        qkv_ref[...] = project().astype(qkv_ref.dtype)

    @pl.when(j >= n_qkv_tiles)
    def _rest():
        rest_ref[...] = project()


def _in_proj(x2d, pos2d, invf, norm_w, w_in_t, *, tm, tn):
    t = x2d.shape[0]
    n = w_in_t.shape[0]
    n_qkv = 3 * ATTN_WIDTH
    nqt = n_qkv // tn
    grid = (t // tm, pl.cdiv(n, tn))
    kern = functools.partial(_in_proj_kernel, tn=tn, n_q_tiles=ATTN_WIDTH // tn,
                             n_rope_tiles=2 * ATTN_WIDTH // tn, n_qkv_tiles=nqt)
    return pl.pallas_call(
        kern,
        grid=grid,
        in_specs=[
            pl.BlockSpec((tm, 1), lambda i, j: (i, 0)),
            pl.BlockSpec((1, LANES), lambda i, j: (0, 0)),
            pl.BlockSpec((tm, D_MODEL), lambda i, j: (i, 0)),
            pl.BlockSpec((1, D_MODEL), lambda i, j: (0, 0)),
            pl.BlockSpec((tn, D_MODEL), lambda i, j: (j, 0)),
        ],
        out_specs=[
            pl.BlockSpec((tm, tn), lambda i, j: (i, jnp.minimum(j, nqt - 1))),
            pl.BlockSpec((tm, tn), lambda i, j: (i, jnp.maximum(j - nqt, 0))),
        ],
        out_shape=[jax.ShapeDtypeStruct((t, n_qkv), BF16),
                   jax.ShapeDtypeStruct((t, n - n_qkv), F32)],
        scratch_shapes=[
            pltpu.VMEM((tm, D_MODEL), BF16),
            pltpu.VMEM((tm, LANES), F32),
            pltpu.VMEM((tm, LANES), F32),
            pltpu.VMEM((tm, LANES), F32),
        ],
        compiler_params=pltpu.CompilerParams(
            dimension_semantics=("parallel", "arbitrary"), vmem_limit_bytes=VMEM_LIMIT),
        name="in_proj",
    )(pos2d, invf, x2d, norm_w, w_in_t)


def _attn_kernel(lq1_ref, lk1_ref, lq2_ref, lk2_ref, q_ref, k_ref, v_ref, w_ref, o_ref, vt_ref,
                 *, tq, tk, gw, hp):
    i = pl.program_id(2)

    @pl.when(i == 0)
    def _transpose_values():
        vt_ref[...] = v_ref[...].astype(F32).T.astype(BF16)

    sub = lax.broadcasted_iota(I32, (ATTN_DV, tq), 0)
    qts = []
    for h in range(hp):
        qt = q_ref[:, h * ATTN_DV:(h + 1) * ATTN_DV].astype(F32).T
        qts.append(jnp.concatenate([jnp.where(sub < ATTN_DH, qt, 0.0),
                                    jnp.where(sub < ATTN_DH, 0.0, qt)], axis=1).astype(BF16))

    ngroups = 2 * tq // gw
    gaps = [lax.broadcasted_iota(I32, (tk, gw), 0)
            - (lax.broadcasted_iota(I32, (tk, gw), 1) + g * gw) % tq for g in range(ngroups)]
    nfull = i * (tq // tk)

    def block(n, stats, diag):
        off = pl.multiple_of(n * tk, tk)
        limit = i * tq - n * tk
        live = [g for g in range(ngroups)
                if diag is None or (g * gw) % tq + gw > diag * tk]
        out = []
        for h in range(hp):
            kb = k_ref[pl.ds(off, tk), h * ATTN_DV:(h + 1) * ATTN_DV]
            vtb = vt_ref[h * ATTN_DV:(h + 1) * ATTN_DV, pl.ds(off, tk)]
            ss = {g: jnp.dot(kb, qts[h][:, g * gw:(g + 1) * gw], preferred_element_type=F32)
                  for g in live}
            mid = {}
            for g in live:
                m, l, _ = stats[h][g]
                s = ss[g]
                if diag is not None and (g * gw) % tq < diag * tk + tk - 1:
                    s = jnp.where(gaps[g] <= limit, s, -jnp.inf)
                m_new = jnp.maximum(m, jnp.max(s, axis=0, keepdims=True))
                a = jnp.exp(m - m_new)
                p = jnp.exp(s - m_new)
                mid[g] = (m_new, a * l + jnp.sum(p, axis=0, keepdims=True), a, p.astype(BF16))
            groups = []
            for g in range(ngroups):
                if g not in mid:
                    groups.append(stats[h][g])
                    continue
                m_new, l, a, p = mid[g]
                acc = a * stats[h][g][2] + jnp.dot(vtb, p, preferred_element_type=F32)
                groups.append((m_new, l, acc))
            out.append(tuple(groups))
        return tuple(out)

    init = tuple(tuple((jnp.full((1, gw), -jnp.inf, F32), jnp.zeros((1, gw), F32),
                        jnp.zeros((ATTN_DV, gw), F32)) for _ in range(ngroups)) for _ in range(hp))
    bpt = tq // tk

    def several(first, stats, masked):
        for r in range(bpt):
            stats = block(first + r, stats, r if masked else None)
        return stats

    stats = lax.fori_loop(0, i, lambda t, c: several(t * bpt, c, False), init)
    stats = several(nfull, stats, True)

    lam = (jnp.exp(jnp.sum(lq1_ref[...] * lk1_ref[...], axis=-1, keepdims=True))
           - jnp.exp(jnp.sum(lq2_ref[...] * lk2_ref[...], axis=-1, keepdims=True)) + LAMBDA_INIT)
    for h in range(hp):
        l = jnp.concatenate([stats[h][g][1] for g in range(ngroups)], axis=1)
        acc = jnp.concatenate([stats[h][g][2] for g in range(ngroups)], axis=1)
        o = acc / l
        d = o[:, :tq] - lam * o[:, tq:]
        ms = jnp.mean(d * d, axis=0, keepdims=True)
        y = (d * lax.rsqrt(ms + EPS) * w_ref[...]) * (1.0 - LAMBDA_INIT)
        o_ref[:, h * ATTN_DV:(h + 1) * ATTN_DV] = y.T.astype(o_ref.dtype)


def _attention(qkv, lq1, lk1, lq2, lk2, subln_w, *, bsz, seq, tq, tk, gw, hp):
    nq = seq // tq
    bw = hp * ATTN_DV
    kb0 = OFF_K // bw
    vb0 = OFF_V // bw
    vec = pl.BlockSpec((1, ATTN_DH), lambda b, h, i: (0, 0))
    return pl.pallas_call(
        functools.partial(_attn_kernel, tq=tq, tk=tk, gw=gw, hp=hp),
        grid=(bsz, ATTN_HEADS // hp, nq),
        in_specs=[
            vec, vec, vec, vec,
            pl.BlockSpec((tq, bw), lambda b, h, i: (b * nq + i, h)),
            pl.BlockSpec((seq, bw), lambda b, h, i: (b, kb0 + h)),
            pl.BlockSpec((seq, bw), lambda b, h, i: (b, vb0 + h)),
            pl.BlockSpec((ATTN_DV, 1), lambda b, h, i: (0, 0)),
        ],
        out_specs=pl.BlockSpec((tq, bw), lambda b, h, i: (b * nq + i, h)),
        out_shape=jax.ShapeDtypeStruct((bsz * seq, ATTN_WIDTH), BF16),
        scratch_shapes=[pltpu.VMEM((bw, seq), BF16)],
        compiler_params=pltpu.CompilerParams(
            dimension_semantics=("parallel", "parallel", "arbitrary"), vmem_limit_bytes=VMEM_LIMIT),
        name="diff_attention",
    )(lq1, lk1, lq2, lk2, qkv, qkv, qkv, subln_w)


def _ssd_kernel(z_ref, xs_ref, b_ref, c_ref, dt_ref, cw_ref, cb_ref, dtb_ref, alog_ref,
                dskip_ref, nw_ref, e_ref, o_ref, ext_ref, state_ref, *, nsub):
    ci = pl.program_id(1)

    @pl.when(ci == 0)
    def _init():
        ext_ref[0:SUBLANES, :] = jnp.zeros((SUBLANES, CONV_DIM), F32)
        state_ref[...] = jnp.zeros_like(state_ref)

    for sc in range(nsub):
        rs = pl.ds(sc * CHUNK, CHUNK)
        _ssd_chunk(z_ref.at[rs], xs_ref.at[rs], b_ref.at[rs], c_ref.at[rs], dt_ref.at[rs], cw_ref, cb_ref,
                   dtb_ref, alog_ref, dskip_ref, nw_ref, e_ref, o_ref.at[rs], ext_ref, state_ref)


def _ssd_chunk(z_ref, xs_ref, b_ref, c_ref, dt_ref, cw_ref, cb_ref, dtb_ref, alog_ref,
               dskip_ref, nw_ref, e_ref, o_ref, ext_ref, state_ref):
    q = CHUNK
    gw = SSD_WIDTH // SSD_GROUPS

    u = jnp.concatenate([xs_ref[...], b_ref[...], c_ref[...]], axis=1)
    tail = ext_ref[0:SUBLANES, :]
    row8 = lax.broadcasted_iota(I32, (SUBLANES, CONV_DIM), 0)
    acc = cb_ref[...] + u * cw_ref[CONV_W - 1:CONV_W, :]
    for w in range(CONV_W - 1):
        sh = CONV_W - 1 - w
        rolled = pltpu.roll(u, sh, 0)
        head = jnp.where(row8 < sh, pltpu.roll(tail, sh, 0), rolled[0:SUBLANES, :])
        acc = acc + jnp.concatenate([head, rolled[SUBLANES:, :]], axis=0) * cw_ref[w:w + 1, :]
    ext_ref[0:SUBLANES, :] = u[q - SUBLANES:q, :]
    xbc = acc * _sigmoid(acc)
    xc = xbc[:, :SSD_WIDTH]
    bm = xbc[:, SSD_WIDTH:SSD_WIDTH + SSD_GROUPS * SSD_STATE]
    cm = xbc[:, SSD_WIDTH + SSD_GROUPS * SSD_STATE:]

    lane = lax.broadcasted_iota(I32, (q, LANES), 1)
    head_ok = lane < SSD_HEADS
    dt = jnp.where(head_ok, _softplus(jnp.where(head_ok, dt_ref[...], 0.0) + dtb_ref[...]), 0.0)
    da = dt * (-jnp.exp(alog_ref[...]))
    row = lax.broadcasted_iota(I32, (q, q), 0)
    col = lax.broadcasted_iota(I32, (q, q), 1)
    causal = row >= col
    cs = jnp.dot(causal.astype(F32), da, preferred_element_type=F32,
                 precision=lax.Precision.HIGHEST)
    cs_t = cs.T
    cs_last = cs[q - 1:q, :]

    stk = jnp.concatenate([dt, jnp.exp(cs), jnp.exp(cs_last - cs)], axis=0)
    hi = stk.astype(BF16)
    lo = (stk - hi.astype(F32)).astype(BF16)
    ex = (jnp.dot(hi, e_ref[...], preferred_element_type=F32)
          + jnp.dot(lo, e_ref[...], preferred_element_type=F32))
    dtx = ex[0:q]
    ecx = ex[q:2 * q]
    dsx = ex[2 * q:3 * q]

    xdt = xc * dtx
    xdt_b = xdt.astype(BF16)
    xw_b = (xdt * dsx).astype(BF16)
    lane_q = lax.broadcasted_iota(I32, (q, LANES), 1)

    y_groups = []
    for g in range(SSD_GROUPS):
        gs = slice(g * gw, (g + 1) * gw)
        bg = bm[:, g * SSD_STATE:(g + 1) * SSD_STATE]
        cg_b = cm[:, g * SSD_STATE:(g + 1) * SSD_STATE].astype(BF16)
        cb = lax.dot_general(cg_b, bg.astype(BF16), (((1,), (1,)), ((), ())),
                             preferred_element_type=F32)
        prev_t = state_ref[g]
        y_off = jnp.dot(cg_b, prev_t.astype(BF16), preferred_element_type=F32) * ecx[:, gs]
        st_t = jnp.dot(bg.T.astype(BF16), xw_b[:, gs], preferred_element_type=F32)
        state_ref[g] = prev_t * ecx[q - 1:q, gs] + st_t

        pairs = []
        for jp in range(gw // LANES):
            rhs = xdt_b[:, g * gw + jp * LANES: g * gw + (jp + 1) * LANES]
            outs = []
            for hh in range(2):
                h = g * (SSD_HEADS // SSD_GROUPS) + jp * 2 + hh
                seg = (jnp.broadcast_to(cs[:, h:h + 1], (q, q))
                       - jnp.broadcast_to(cs_t[h:h + 1, :], (q, q)))
                dec = jnp.exp(jnp.where(causal, seg, -jnp.inf))
                outs.append(jnp.dot((cb * dec).astype(BF16), rhs, preferred_element_type=F32))
            pairs.append(jnp.where(lane_q < SSD_HEADDIM, outs[0], outs[1]))
        y_groups.append(jnp.concatenate(pairs, axis=1) + y_off)

    y = jnp.concatenate(y_groups, axis=1) + dskip_ref[...] * xc
    z = z_ref[...]
    y = y * (z * _sigmoid(z))
    outs = []
    for g in range(SSD_GROUPS):
        yg = y[:, g * gw:(g + 1) * gw]
        outs.append(yg * lax.rsqrt(jnp.mean(yg * yg, axis=-1, keepdims=True) + EPS))
    o_ref[...] = (jnp.concatenate(outs, axis=1) * nw_ref[...]).astype(o_ref.dtype)


def _ssd(proj, conv_w, conv_b, dt_bias_p, a_log_p, dskip_x, ssd_norm_w, expand, *, bsz, seq, nsub):
    rows = nsub * CHUNK
    nc = seq // rows
    sw = SSD_GROUPS * SSD_STATE
    const = lambda shape: pl.BlockSpec(shape, lambda b, c: (0, 0))
    return pl.pallas_call(
        functools.partial(_ssd_kernel, nsub=nsub),
        grid=(bsz, nc),
        in_specs=[
            pl.BlockSpec((rows, SSD_WIDTH), lambda b, c: (b * nc + c, OFF_Z // SSD_WIDTH)),
            pl.BlockSpec((rows, SSD_WIDTH), lambda b, c: (b * nc + c, OFF_X // SSD_WIDTH)),
            pl.BlockSpec((rows, sw), lambda b, c: (b * nc + c, OFF_B // sw)),
            pl.BlockSpec((rows, sw), lambda b, c: (b * nc + c, OFF_C // sw)),
            pl.BlockSpec((rows, LANES), lambda b, c: (b * nc + c, OFF_DT // LANES)),
            const((CONV_W, CONV_DIM)),
            const((1, CONV_DIM)),
            const((1, LANES)),
            const((1, LANES)),
            const((1, SSD_WIDTH)),
            const((1, SSD_WIDTH)),
            const((LANES, SSD_WIDTH)),
        ],
        out_specs=pl.BlockSpec((rows, SSD_WIDTH), lambda b, c: (b * nc + c, 0)),
        out_shape=jax.ShapeDtypeStruct((bsz * seq, SSD_WIDTH), BF16),
        scratch_shapes=[
            pltpu.VMEM((SUBLANES, CONV_DIM), F32),
            pltpu.VMEM((SSD_GROUPS, SSD_STATE, SSD_WIDTH // SSD_GROUPS), F32),
        ],
        compiler_params=pltpu.CompilerParams(
            dimension_semantics=("parallel", "arbitrary"), vmem_limit_bytes=VMEM_LIMIT),
        name="ssd",
    )(proj, proj, proj, proj, proj, conv_w, conv_b, dt_bias_p, a_log_p, dskip_x, ssd_norm_w, expand)


def _out_proj_kernel(x_ref, a_ref, s_ref, wo_ref, n2_ref, rw_ref, rb_ref, x2_ref, h2_ref, lg_ref,
                     *, sub):
    for r0 in range(0, x_ref.shape[0], sub):
        rs = slice(r0, r0 + sub)
        cat = jnp.concatenate([a_ref[rs, :], s_ref[rs, :]], axis=1)
        x2 = x_ref[rs, :] + jnp.dot(cat, wo_ref[...], preferred_element_type=F32)
        x2_ref[rs, :] = x2
        ms = jnp.mean(x2 * x2, axis=-1, keepdims=True)
        h2 = x2 * lax.rsqrt(ms + EPS) * n2_ref[...]
        h2_ref[rs, :] = h2
        h_hi = h2.astype(BF16)
        h_mid = (h2 - h_hi.astype(F32)).astype(BF16)
        p1 = jnp.dot(h_hi, rw_ref[...], preferred_element_type=F32)
        p2 = jnp.dot(h_mid, rw_ref[:, 0:LANES], preferred_element_type=F32)
        lg = p1[:, 0:LANES] + p1[:, LANES:] + p2
        lg_ref[:, rs] = lg.T[0:N_EXPERTS, :] + rb_ref[...]


def _out_proj(x2d, attn, ssd, w_out_b, norm2_w, router_w2, router_b, *, tm, sub):
    t = x2d.shape[0]
    const = lambda shape: pl.BlockSpec(shape, lambda i: (0, 0))
    return pl.pallas_call(
        functools.partial(_out_proj_kernel, sub=sub),
        grid=(t // tm,),
        in_specs=[
            pl.BlockSpec((tm, D_MODEL), lambda i: (i, 0)),
            pl.BlockSpec((tm, ATTN_WIDTH), lambda i: (i, 0)),
            pl.BlockSpec((tm, SSD_WIDTH), lambda i: (i, 0)),
            const((D_MODEL, D_MODEL)),
            const((1, D_MODEL)),
            const((D_MODEL, 2 * LANES)),
            const((N_EXPERTS, 1)),
        ],
        out_specs=[
            pl.BlockSpec((tm, D_MODEL), lambda i: (i, 0)),
            pl.BlockSpec((tm, D_MODEL), lambda i: (i, 0)),
            pl.BlockSpec((N_EXPERTS, tm), lambda i: (0, i)),
        ],
        out_shape=[
            jax.ShapeDtypeStruct((t, D_MODEL), F32),
            jax.ShapeDtypeStruct((t, D_MODEL), F32),
            jax.ShapeDtypeStruct((N_EXPERTS, t), F32),
        ],
        compiler_params=pltpu.CompilerParams(
            dimension_semantics=("parallel",), vmem_limit_bytes=VMEM_LIMIT),
        name="out_proj",
    )(x2d, attn, ssd, w_out_b, norm2_w, router_w2, router_b)


def _route_kernel(lg_ref, idx_ref, gate_ref, pos_ref, cnt_ref, off_ref, rank_ref, *, tt, row_align):
    t_total = lg_ref.shape[1]
    nt = t_total // tt
    eio = lax.broadcasted_iota(I32, (N_EXPERTS, tt), 0)
    r = lax.broadcasted_iota(I32, (tt, tt), 0)
    c = lax.broadcasted_iota(I32, (tt, tt), 1)
    before = jnp.where(r < c, 1.0, 0.0).astype(BF16)

    def tile(ti, cnt):
        off = pl.multiple_of(ti * tt, tt)
        work = lg_ref[:, pl.ds(off, tt)]
        vals, hots = [], []
        for k in range(TOP_K):
            m = jnp.max(work, axis=0, keepdims=True)
            ik = jnp.min(jnp.where(work == m, eio, N_EXPERTS), axis=0, keepdims=True)
            hot = eio == ik
            work = jnp.where(hot, -jnp.inf, work)
            idx_ref[k:k + 1, pl.ds(off, tt)] = ik
            vals.append(m)
            hots.append(hot)
        es = [jnp.exp(v - vals[0]) for v in vals]
        inv = 1.0 / (es[0] + es[1] + es[2] + es[3])
        sel = jnp.zeros((N_EXPERTS, tt), F32)
        for k in range(TOP_K):
            gate_ref[k:k + 1, pl.ds(off, tt)] = es[k] * inv
            sel = sel + jnp.where(hots[k], 1.0, 0.0)
        rank = jnp.dot(sel.astype(BF16), before, preferred_element_type=F32) + cnt
        for k in range(TOP_K):
            rank_ref[k:k + 1, pl.ds(off, tt)] = jnp.sum(jnp.where(hots[k], rank, 0.0), axis=0, keepdims=True)
        return cnt + jnp.sum(sel, axis=1, keepdims=True)

    cnt = lax.fori_loop(0, nt, tile, jnp.zeros((N_EXPERTS, 1), F32))
    cnt_ref[...] = cnt.astype(I32)
    padded = jnp.ceil(cnt * (1.0 / row_align)) * row_align
    er = lax.broadcasted_iota(I32, (N_EXPERTS, N_EXPERTS), 0)
    ec = lax.broadcasted_iota(I32, (N_EXPERTS, N_EXPERTS), 1)
    lower = jnp.where(ec < er, 1.0, 0.0)
    offs = jnp.dot(lower, jnp.broadcast_to(padded, (N_EXPERTS, LANES)), preferred_element_type=F32,
                   precision=lax.Precision.HIGHEST)[:, 0:1]
    off_ref[...] = offs.astype(I32)

    def place(ti, carry):
        off = pl.multiple_of(ti * tt, tt)
        for k in range(TOP_K):
            hot = eio == idx_ref[k:k + 1, pl.ds(off, tt)]
            base = jnp.sum(jnp.where(hot, offs, 0.0), axis=0, keepdims=True)
            pos_ref[k:k + 1, pl.ds(off, tt)] = (base + rank_ref[k:k + 1, pl.ds(off, tt)]).astype(I32)
        return carry

    lax.fori_loop(0, nt, place, 0)


def _route(logits_t, *, tt, row_align):
    t = logits_t.shape[1]
    full = lambda shape: pl.BlockSpec(shape, lambda i: (0, 0))
    return pl.pallas_call(
        functools.partial(_route_kernel, tt=tt, row_align=row_align),
        grid=(1,),
        in_specs=[full((N_EXPERTS, t))],
        out_specs=[full((TOP_K, t)), full((TOP_K, t)), full((TOP_K, t)),
                   full((N_EXPERTS, 1)), full((N_EXPERTS, 1))],
        out_shape=[
            jax.ShapeDtypeStruct((TOP_K, t), I32),
            jax.ShapeDtypeStruct((TOP_K, t), F32),
            jax.ShapeDtypeStruct((TOP_K, t), I32),
            jax.ShapeDtypeStruct((N_EXPERTS, 1), I32),
            jax.ShapeDtypeStruct((N_EXPERTS, 1), I32),
        ],
        scratch_shapes=[pltpu.VMEM((TOP_K, t), F32)],
        compiler_params=pltpu.CompilerParams(
            dimension_semantics=("arbitrary",), vmem_limit_bytes=VMEM_LIMIT),
        name="route",
    )(logits_t)


def _dispatch_kernel(pos_ref, h_ref, xs_ref, sem, *, tm, t_total):
    i = pl.program_id(0)

    def issue(tb, carry):
        for tu in range(ISSUE_UNROLL):
            t = pl.multiple_of(tb * ISSUE_UNROLL, ISSUE_UNROLL) + tu
            for k in range(TOP_K):
                p = pos_ref[k * t_total + i * tm + t]
                pltpu.make_async_copy(h_ref.at[pl.ds(t, 1), :], xs_ref.at[pl.ds(p, 1), :], sem).start()
        return carry

    lax.fori_loop(0, tm // ISSUE_UNROLL, issue, 0)
    for k in range(TOP_K):
        pltpu.make_async_copy(h_ref, xs_ref.at[pl.ds(0, tm), :], sem).wait()


def _dispatch(pos_flat, h2, *, tm, rows_alloc):
    t = h2.shape[0]
    return pl.pallas_call(
        functools.partial(_dispatch_kernel, tm=tm, t_total=t),
        grid_spec=pltpu.PrefetchScalarGridSpec(
            num_scalar_prefetch=1,
            grid=(t // tm,),
            in_specs=[pl.BlockSpec((tm, D_MODEL), lambda i, pos: (i, 0))],
            out_specs=pl.BlockSpec(memory_space=pl.ANY),
            scratch_shapes=[pltpu.SemaphoreType.DMA(())],
        ),
        out_shape=jax.ShapeDtypeStruct((rows_alloc, D_MODEL), F32),
        compiler_params=pltpu.CompilerParams(
            dimension_semantics=("arbitrary",), vmem_limit_bytes=VMEM_LIMIT),
        name="dispatch",
    )(pos_flat, h2)


def _experts_kernel(se_ref, ss_ref, sr_ref, sv_ref, xs_ref, wg_ref, wu_ref, wd_ref, bg_ref, bu_ref,
                    bd_ref, ys_ref, xseg, yacc, sem_in, sem_out, *, unit, max_units, nj):
    s = pl.program_id(0)
    j = pl.program_id(1)
    nseg = pl.num_programs(0)
    rows = sr_ref[s]
    start = ss_ref[s]
    sliver = unit // 4
    ahead = 4

    def plan(r):
        nu = (r + unit - 1) // unit
        return nu, (nu % 2 == 1) & (nu >= 3) & (r - (nu - 1) * unit <= sliver)

    nunits, fuse = plan(rows)
    odd = nunits % 2 == 1
    nbig = nunits // 2 - fuse.astype(I32)
    prev_units, prev_fuse = plan(sr_ref[jnp.maximum(s - 1, 0)])
    prev_units = jnp.where(s > 0, prev_units, 0)
    nxt = jnp.minimum(s + 1, nseg - 1)
    has_next = (s + 1 < nseg) & (sr_ref[nxt] > 0)
    next_units = jnp.where(has_next, plan(sr_ref[nxt])[0], 0)

    def x_copy(u, base):
        r0 = pl.multiple_of(base + u * unit, SUBLANES)
        l0 = pl.multiple_of(u * unit, unit)
        return pltpu.make_async_copy(xs_ref.at[pl.ds(r0, unit), :], xseg.at[pl.ds(l0, unit), :],
                                     sem_in.at[u])

    def y_copy(u, nrows=unit):
        r0 = pl.multiple_of(start + u * unit, SUBLANES)
        l0 = pl.multiple_of(u * unit, unit)
        return pltpu.make_async_copy(yacc.at[pl.ds(l0, nrows), :], ys_ref.at[pl.ds(r0, nrows), :],
                                     sem_out.at[u])

    def fetch(u):
        @pl.when((u < nunits) & (u >= prev_units))
        def _():
            x_copy(u, start).start()

    def await_copy_out(u, n_units, fused):
        @pl.when((u < n_units) & jnp.logical_not(fused & (u == n_units - 1)))
        def _whole():
            y_copy(u).wait()

        @pl.when((u < n_units) & fused & (u == n_units - 1))
        def _part():
            y_copy(u, sliver).wait()

    def chunk(u0, m):
        nu = -(-m // unit)
        l0 = pl.multiple_of(u0 * unit, unit)

        @pl.when(j == 0)
        def _arrived():
            if m == 2 * unit:
                for k in range(2):
                    fetch(u0 + ahead + k)
            for k in range(nu):
                x_copy(u0 + k, start).wait()
                await_copy_out(u0 + k, prev_units, prev_fuse)

        x = xseg[pl.ds(l0, m), :].astype(BF16)
        g = jnp.dot(x, wg_ref[...].astype(BF16), preferred_element_type=F32) + bg_ref[...]
        u = jnp.dot(x, wu_ref[...].astype(BF16), preferred_element_type=F32) + bu_ref[...]
        g = jnp.minimum(g, SWIGLU_LIMIT)
        u = jnp.clip(u, -SWIGLU_LIMIT, SWIGLU_LIMIT)
        act = (g * _sigmoid(SWIGLU_ALPHA * g) * (u + 1.0)).astype(BF16)

        @pl.when(j == 0)
        def _first():
            yacc[pl.ds(l0, m), :] = bd_ref[...] + jnp.dot(
                act, wd_ref[...].astype(BF16), preferred_element_type=F32)

        @pl.when(j > 0)
        def _rest():
            yacc[pl.ds(l0, m), :] = yacc[pl.ds(l0, m), :] + jnp.dot(
                act, wd_ref[...].astype(BF16), preferred_element_type=F32)

        @pl.when(j == nj - 1)
        def _flush():
            for k in range(m // unit):
                y_copy(u0 + k).start()
            if m % unit:
                y_copy(u0 + m // unit, m % unit).start()
            for k in range(nu):
                @pl.when(u0 + k < next_units)
                def _():
                    x_copy(u0 + k, ss_ref[nxt]).start()

    @pl.when(rows > 0)
    def _work():
        @pl.when(j == 0)
        def _begin():
            for u in range(max_units):
                @pl.when(u >= nunits)
                def _():
                    await_copy_out(u, prev_units, prev_fuse)

            for u in range(ahead):
                fetch(u)

        def big(c, carry):
            chunk(2 * c, 2 * unit)
            return carry

        lax.fori_loop(0, nbig, big, 0)

        @pl.when(fuse)
        def _with_sliver():
            chunk(2 * nbig, 2 * unit + sliver)

        @pl.when(odd & jnp.logical_not(fuse))
        def _tail():
            chunk(2 * nbig, unit)

        @pl.when((j == nj - 1) & jnp.logical_not(has_next))
        def _final():
            for u in range(max_units):
                await_copy_out(u, nunits, fuse)


def _experts(seg_e, seg_start, seg_rows, seg_valid, xs, w_gate_up, b_gate_up, w_down, b_down,
             *, unit, tf, cap, nseg):
    nj = D_FF // tf
    rows_alloc = xs.shape[0]

    def jsel(j, sv, s):
        return j * sv[s] + (nj - 1) * (1 - sv[s])

    return pl.pallas_call(
        functools.partial(_experts_kernel, unit=unit, max_units=cap // unit, nj=nj),
        grid_spec=pltpu.PrefetchScalarGridSpec(
            num_scalar_prefetch=4,
            grid=(nseg, nj),
            in_specs=[
                pl.BlockSpec(memory_space=pl.ANY),
                pl.BlockSpec((None, D_MODEL, tf), lambda s, j, se, ss, sr, sv: (se[s], 0, jsel(j, sv, s))),
                pl.BlockSpec((None, D_MODEL, tf), lambda s, j, se, ss, sr, sv: (se[s], 0, nj + jsel(j, sv, s))),
                pl.BlockSpec((None, tf, D_MODEL), lambda s, j, se, ss, sr, sv: (se[s], jsel(j, sv, s), 0)),
                pl.BlockSpec((None, 1, tf), lambda s, j, se, ss, sr, sv: (se[s], 0, jsel(j, sv, s))),
                pl.BlockSpec((None, 1, tf), lambda s, j, se, ss, sr, sv: (se[s], 0, nj + jsel(j, sv, s))),
                pl.BlockSpec((None, 1, D_MODEL), lambda s, j, se, ss, sr, sv: (se[s], 0, 0)),
            ],
            out_specs=pl.BlockSpec(memory_space=pl.ANY),
            scratch_shapes=[
                pltpu.VMEM((cap, D_MODEL), F32),
                pltpu.VMEM((cap, D_MODEL), F32),
                pltpu.SemaphoreType.DMA((cap // unit,)),
                pltpu.SemaphoreType.DMA((cap // unit,)),
            ],
        ),
        out_shape=jax.ShapeDtypeStruct((rows_alloc, D_MODEL), F32),
        compiler_params=pltpu.CompilerParams(
            dimension_semantics=("arbitrary", "arbitrary"), vmem_limit_bytes=VMEM_LIMIT),
        name="experts",
    )(seg_e, seg_start, seg_rows, seg_valid, xs, w_gate_up, w_gate_up, w_down,
      b_gate_up, b_gate_up, b_down)


def _combine_kernel(pos_ref, x2_ref, gate_ref, nf_ref, ys_ref, o_ref, buf, sem, *, tm, t_total):
    i = pl.program_id(0)
    n = pl.num_programs(0)

    def gather(tile, slot):
        def issue(tb, carry):
            for tu in range(ISSUE_UNROLL):
                t = pl.multiple_of(tb * ISSUE_UNROLL, ISSUE_UNROLL) + tu
                for k in range(TOP_K):
                    p = pos_ref[k * t_total + tile * tm + t]
                    pltpu.make_async_copy(ys_ref.at[pl.ds(p, 1), :], buf.at[slot, k, pl.ds(t, 1), :],
                                          sem.at[slot]).start()
            return carry

        lax.fori_loop(0, tm // ISSUE_UNROLL, issue, 0)

    @pl.when(i == 0)
    def _first():
        gather(0, 0)

    @pl.when(i + 1 < n)
    def _ahead():
        gather(i + 1, (i + 1) % 2)

    slot = i % 2
    for k in range(TOP_K):
        pltpu.make_async_copy(ys_ref.at[pl.ds(0, tm), :], buf.at[slot, k], sem.at[slot]).wait()

    g = gate_ref[...]
    x = x2_ref[...]
    for k in range(TOP_K):
        x = x + g[:, k:k + 1] * buf[slot, k]
    ms = jnp.mean(x * x, axis=-1, keepdims=True)
    o_ref[...] = x * lax.rsqrt(ms + EPS) * nf_ref[...]


def _combine(pos_flat, x2, gates_t, norm_f_w, ys, *, tm):
    t = x2.shape[0]
    return pl.pallas_call(
        functools.partial(_combine_kernel, tm=tm, t_total=t),
        grid_spec=pltpu.PrefetchScalarGridSpec(
            num_scalar_prefetch=1,
            grid=(t // tm,),
            in_specs=[
                pl.BlockSpec((tm, D_MODEL), lambda i, pos: (i, 0)),
                pl.BlockSpec((tm, TOP_K), lambda i, pos: (i, 0)),
                pl.BlockSpec((1, D_MODEL), lambda i, pos: (0, 0)),
                pl.BlockSpec(memory_space=pl.ANY),
            ],
            out_specs=pl.BlockSpec((tm, D_MODEL), lambda i, pos: (i, 0)),
            scratch_shapes=[pltpu.VMEM((2, TOP_K, tm, D_MODEL), F32), pltpu.SemaphoreType.DMA((2,))],
        ),
        out_shape=jax.ShapeDtypeStruct((t, D_MODEL), F32),
        compiler_params=pltpu.CompilerParams(
            dimension_semantics=("arbitrary",), vmem_limit_bytes=VMEM_LIMIT),
        name="combine",
    )(pos_flat, x2, gates_t, norm_f_w, ys)


def _tiles(t):
    return dict(
        in_tm=min(1024, t), in_tn=512,
        attn_tq=512, attn_tk=256, attn_gw=128, attn_hp=8,
        out_tm=min(512, t), out_sub=256,
        ssd_nsub=4,
        route_tt=min(512, t),
        disp_tm=min(256, t),
        moe_unit=256, moe_tf=512, moe_cap=min(1536, TOP_K * t),
        comb_tm=min(256, t),
    )


def _segments(cnt, offs, *, cap, nseg):
    nseg_e = (cnt + cap - 1) // cap
    ends = jnp.cumsum(nseg_e)
    total = ends[-1]
    s = jnp.arange(nseg, dtype=I32)
    valid = s < total
    s_eff = jnp.minimum(s, jnp.maximum(total - 1, 0))
    e = jnp.minimum(jnp.sum((s_eff[:, None] >= ends[None, :]).astype(I32), axis=1), N_EXPERTS - 1)
    local = s_eff - (ends[e] - nseg_e[e])
    start = offs[e] + local * cap
    rows = jnp.where(valid, jnp.clip(cnt[e] - local * cap, 0, cap), 0)
    return e, start.astype(I32), rows.astype(I32), valid.astype(I32)


def kernel(x, positions, norm1_w, w_in, lambda_q1, lambda_k1, lambda_q2, lambda_k2, attn_subln_w,
           conv_w, conv_b, dt_bias, a_log, d_skip, ssd_norm_w, w_out, norm2_w, router_w, router_b,
           w_gate_up, b_gate_up, w_down, b_down, norm_f_w):
    bsz, seq, _ = x.shape
    t = bsz * seq
    depth = w_in.shape[0]
    assert depth == 1, "single-layer block"
    cfg = _tiles(t)
    layer = 0

    x2d = x.reshape(t, D_MODEL)
    pos2d = positions.reshape(t, 1).astype(I32)
    inv_freq = ROPE_THETA ** (-jnp.arange(0, ROT_DIM, 2, dtype=F32) / ROT_DIM)
    d = jnp.arange(LANES) % ATTN_DH
    invf = jnp.where(d < ROT_DIM, inv_freq[d % ROT_HALF], 0.0).reshape(1, LANES).astype(F32)

    qkv, proj = _in_proj(x2d, pos2d, invf, norm1_w[layer].reshape(1, D_MODEL), w_in[layer].T,
                         tm=cfg["in_tm"], tn=cfg["in_tn"])

    vec = lambda a: a[layer].reshape(1, -1).astype(F32)
    attn = _attention(qkv, vec(lambda_q1), vec(lambda_k1), vec(lambda_q2), vec(lambda_k2),
                      attn_subln_w[layer].astype(F32).reshape(ATTN_DV, 1),
                      bsz=bsz, seq=seq, tq=min(cfg["attn_tq"], seq),
                      tk=min(cfg["attn_tk"], seq), gw=cfg["attn_gw"], hp=cfg["attn_hp"])

    pad_heads = lambda a: jnp.pad(a[layer].astype(F32), (0, LANES - SSD_HEADS)).reshape(1, LANES)
    head_of_lane = jnp.arange(SSD_WIDTH) // SSD_HEADDIM
    expand = (jnp.arange(LANES)[:, None] == head_of_lane[None, :]).astype(BF16)
    dskip_x = jnp.repeat(d_skip[layer].astype(F32), SSD_HEADDIM).reshape(1, SSD_WIDTH)
    ssd = _ssd(proj, conv_w[layer], conv_b[layer].reshape(1, CONV_DIM), pad_heads(dt_bias),
               pad_heads(a_log), dskip_x, ssd_norm_w[layer].reshape(1, SSD_WIDTH), expand,
               bsz=bsz, seq=seq, nsub=cfg["ssd_nsub"])

    rw = router_w[layer].astype(F32)
    rw_hi = rw.astype(BF16)
    rw_mid = (rw - rw_hi.astype(F32)).astype(BF16)
    lane_pad = lambda a: jnp.pad(a, ((0, 0), (0, LANES - N_EXPERTS)))
    router_w2 = jnp.concatenate([lane_pad(rw_hi), lane_pad(rw_mid)], axis=1)
    x2, h2, logits_t = _out_proj(
        x2d, attn, ssd, w_out[layer].astype(BF16), norm2_w[layer].reshape(1, D_MODEL),
        router_w2, router_b[layer].reshape(N_EXPERTS, 1),
        tm=cfg["out_tm"], sub=min(cfg["out_sub"], cfg["out_tm"]))

    idx, gates, pos, cnt, offs = _route(logits_t, tt=cfg["route_tt"], row_align=SUBLANES)
    del idx
    pos_flat = pos.reshape(TOP_K * t)

    cap, unit = cfg["moe_cap"], cfg["moe_unit"]
    nseg = N_EXPERTS + (TOP_K * t) // cap
    rows_alloc = TOP_K * t + N_EXPERTS * SUBLANES + unit
    seg_e, seg_start, seg_rows, seg_valid = _segments(
        cnt.reshape(N_EXPERTS), offs.reshape(N_EXPERTS), cap=cap, nseg=nseg)

    xs = _dispatch(pos_flat, h2, tm=cfg["disp_tm"], rows_alloc=rows_alloc)
    ys = _experts(seg_e, seg_start, seg_rows, seg_valid, xs, w_gate_up[layer],
                  b_gate_up[layer].reshape(N_EXPERTS, 1, 2 * D_FF), w_down[layer],
                  b_down[layer].reshape(N_EXPERTS, 1, D_MODEL),
                  unit=unit, tf=cfg["moe_tf"], cap=cap, nseg=nseg)
    out = _combine(pos_flat, x2, gates.T, norm_f_w.reshape(1, D_MODEL), ys, tm=cfg["comb_tm"])
    return out.reshape(bsz, seq, D_MODEL)
```

```python
import functools
import math

import jax
import jax.numpy as jnp
from jax import lax
from jax.experimental import pallas as pl
from jax.experimental.pallas import tpu as pltpu

F32 = jnp.float32
BF16 = jnp.bfloat16
I32 = jnp.int32

D_MODEL = 2048
ATTN_WIDTH = 1024
SSD_WIDTH = 1024
ATTN_HEADS = 8
ATTN_DH = 64
ATTN_DV = 128
ROT_DIM = 16
ROT_HALF = ROT_DIM // 2
ROPE_THETA = 500000.0
SSD_HEADDIM = 64
SSD_HEADS = 16
SSD_GROUPS = 2
SSD_STATE = 128
CONV_W = 4
CONV_DIM = SSD_WIDTH + 2 * SSD_GROUPS * SSD_STATE
CHUNK = 128
N_EXPERTS = 32
TOP_K = 4
D_FF = 2048
SWIGLU_ALPHA = 1.702
SWIGLU_LIMIT = 7.0
EPS = 1e-5
IN_TOTAL = 3 * ATTN_WIDTH + SSD_WIDTH + CONV_DIM + SSD_HEADS
LAMBDA_INIT = 0.8 - 0.6 * math.exp(-0.3 * 0)

LANES = 128
SUBLANES = 8
VMEM_LIMIT = 56 * 1024 * 1024
ISSUE_UNROLL = 8

OFF_Q = 0
OFF_K = ATTN_WIDTH
OFF_V = 2 * ATTN_WIDTH
OFF_Z = 0
OFF_X = OFF_Z + SSD_WIDTH
OFF_B = OFF_X + SSD_WIDTH
OFF_C = OFF_B + SSD_GROUPS * SSD_STATE
OFF_DT = OFF_C + SSD_GROUPS * SSD_STATE


def _sigmoid(x):
    return 1.0 / (1.0 + jnp.exp(-x))


def _softplus(x):
    return jnp.maximum(x, 0.0) + jnp.log1p(jnp.exp(-jnp.abs(x)))


def _in_proj_kernel(pos_ref, invf_ref, x_ref, nw_ref, w_ref, qkv_ref, rest_ref,
                    hn_ref, c_ref, sa_ref, sb_ref, *, tn, n_q_tiles, n_rope_tiles, n_qkv_tiles):
    j = pl.program_id(1)

    @pl.when(j == 0)
    def _prepare():
        x = x_ref[...]
        ms = jnp.mean(x * x, axis=-1, keepdims=True)
        hn_ref[...] = (x * lax.rsqrt(ms + EPS) * nw_ref[...]).astype(BF16)
        ang = pos_ref[...].astype(F32) * invf_ref[...]
        d = lax.broadcasted_iota(I32, ang.shape, 1) % ATTN_DH
        cos = jnp.cos(ang)
        sin = jnp.sin(ang)
        c_ref[...] = jnp.where(d < ROT_DIM, cos, 1.0)
        sa_ref[...] = jnp.where(d < ROT_HALF, 0.0, jnp.where(d < ROT_DIM, sin, 0.0))
        sb_ref[...] = jnp.where(d < ROT_HALF, -sin, 0.0)

    def project():
        return lax.dot_general(hn_ref[...], w_ref[...].astype(BF16), (((1,), (1,)), ((), ())),
                               preferred_element_type=F32)

    @pl.when(j < n_rope_tiles)
    def _rope():
        acc = project()
        reps = tn // LANES
        c = jnp.tile(c_ref[...], (1, reps))
        sa = jnp.tile(sa_ref[...], (1, reps))
        sb = jnp.tile(sb_ref[...], (1, reps))
        r = acc * c + pltpu.roll(acc, ROT_HALF, 1) * sa + pltpu.roll(acc, tn - ROT_HALF, 1) * sb
        scale = jnp.where(j < n_q_tiles, ATTN_DH ** -0.5, 1.0)
        qkv_ref[...] = (r * scale).astype(qkv_ref.dtype)

    @pl.when((j >= n_rope_tiles) & (j < n_qkv_tiles))
    def _value():
        qkv_ref[...] = project().astype(qkv_ref.dtype)

    @pl.when(j >= n_qkv_tiles)
    def _rest():
        rest_ref[...] = project()


def _in_proj(x2d, pos2d, invf, norm_w, w_in_t, *, tm, tn):
    t = x2d.shape[0]
    n = w_in_t.shape[0]
    n_qkv = 3 * ATTN_WIDTH
    nqt = n_qkv // tn
    grid = (t // tm, pl.cdiv(n, tn))
    kern = functools.partial(_in_proj_kernel, tn=tn, n_q_tiles=ATTN_WIDTH // tn,
                             n_rope_tiles=2 * ATTN_WIDTH // tn, n_qkv_tiles=nqt)
    return pl.pallas_call(
        kern,
        grid=grid,
        in_specs=[
            pl.BlockSpec((tm, 1), lambda i, j: (i, 0)),
            pl.BlockSpec((1, LANES), lambda i, j: (0, 0)),
            pl.BlockSpec((tm, D_MODEL), lambda i, j: (i, 0)),
            pl.BlockSpec((1, D_MODEL), lambda i, j: (0, 0)),
            pl.BlockSpec((tn, D_MODEL), lambda i, j: (j, 0)),
        ],
        out_specs=[
            pl.BlockSpec((tm, tn), lambda i, j: (i, jnp.minimum(j, nqt - 1))),
            pl.BlockSpec((tm, tn), lambda i, j: (i, jnp.maximum(j - nqt, 0))),
        ],
        out_shape=[jax.ShapeDtypeStruct((t, n_qkv), BF16),
                   jax.ShapeDtypeStruct((t, n - n_qkv), F32)],
        scratch_shapes=[
            pltpu.VMEM((tm, D_MODEL), BF16),
            pltpu.VMEM((tm, LANES), F32),
            pltpu.VMEM((tm, LANES), F32),
            pltpu.VMEM((tm, LANES), F32),
        ],
        compiler_params=pltpu.CompilerParams(
            dimension_semantics=("parallel", "arbitrary"), vmem_limit_bytes=VMEM_LIMIT),
        name="in_proj",
    )(pos2d, invf, x2d, norm_w, w_in_t)


def _attn_kernel(lq1_ref, lk1_ref, lq2_ref, lk2_ref, q_ref, k_ref, v_ref, w_ref, o_ref, vt_ref,
                 *, tq, tk, gw, hp):
    i = pl.program_id(2)

    @pl.when(i == 0)
    def _transpose_values():
        vt_ref[...] = v_ref[...].astype(F32).T.astype(BF16)

    sub = lax.broadcasted_iota(I32, (ATTN_DV, tq), 0)
    qts = []
    for h in range(hp):
        qt = q_ref[:, h * ATTN_DV:(h + 1) * ATTN_DV].astype(F32).T
        qts.append(jnp.concatenate([jnp.where(sub < ATTN_DH, qt, 0.0),
                                    jnp.where(sub < ATTN_DH, 0.0, qt)], axis=1).astype(BF16))

    ngroups = 2 * tq // gw
    gaps = [lax.broadcasted_iota(I32, (tk, gw), 0)
            - (lax.broadcasted_iota(I32, (tk, gw), 1) + g * gw) % tq for g in range(ngroups)]
    nfull = i * (tq // tk)

    def block(n, stats, diag):
        off = pl.multiple_of(n * tk, tk)
        limit = i * tq - n * tk
        live = [g for g in range(ngroups)
                if diag is None or (g * gw) % tq + gw > diag * tk]
        out = []
        for h in range(hp):
            kb = k_ref[pl.ds(off, tk), h * ATTN_DV:(h + 1) * ATTN_DV]
            vtb = vt_ref[h * ATTN_DV:(h + 1) * ATTN_DV, pl.ds(off, tk)]
            ss = {g: jnp.dot(kb, qts[h][:, g * gw:(g + 1) * gw], preferred_element_type=F32)
                  for g in live}
            mid = {}
            for g in live:
                m, l, _ = stats[h][g]
                s = ss[g]
                if diag is not None and (g * gw) % tq < diag * tk + tk - 1:
                    s = jnp.where(gaps[g] <= limit, s, -jnp.inf)
                m_new = jnp.maximum(m, jnp.max(s, axis=0, keepdims=True))
                a = jnp.exp(m - m_new)
                p = jnp.exp(s - m_new)
                mid[g] = (m_new, a * l + jnp.sum(p, axis=0, keepdims=True), a, p.astype(BF16))
            groups = []
            for g in range(ngroups):
                if g not in mid:
                    groups.append(stats[h][g])
                    continue
                m_new, l, a, p = mid[g]
                acc = a * stats[h][g][2] + jnp.dot(vtb, p, preferred_element_type=F32)
                groups.append((m_new, l, acc))
            out.append(tuple(groups))
        return tuple(out)

    init = tuple(tuple((jnp.full((1, gw), -jnp.inf, F32), jnp.zeros((1, gw), F32),
                        jnp.zeros((ATTN_DV, gw), F32)) for _ in range(ngroups)) for _ in range(hp))
    bpt = tq // tk

    def several(first, stats, masked):
        for r in range(bpt):
            stats = block(first + r, stats, r if masked else None)
        return stats

    stats = lax.fori_loop(0, i, lambda t, c: several(t * bpt, c, False), init)
    stats = several(nfull, stats, True)

    lam = (jnp.exp(jnp.sum(lq1_ref[...] * lk1_ref[...], axis=-1, keepdims=True))
           - jnp.exp(jnp.sum(lq2_ref[...] * lk2_ref[...], axis=-1, keepdims=True)) + LAMBDA_INIT)
    for h in range(hp):
        l = jnp.concatenate([stats[h][g][1] for g in range(ngroups)], axis=1)
        acc = jnp.concatenate([stats[h][g][2] for g in range(ngroups)], axis=1)
        o = acc / l
        d = o[:, :tq] - lam * o[:, tq:]
        ms = jnp.mean(d * d, axis=0, keepdims=True)
        y = (d * lax.rsqrt(ms + EPS) * w_ref[...]) * (1.0 - LAMBDA_INIT)
        o_ref[:, h * ATTN_DV:(h + 1) * ATTN_DV] = y.T.astype(o_ref.dtype)


def _attention(qkv, lq1, lk1, lq2, lk2, subln_w, *, bsz, seq, tq, tk, gw, hp):
    nq = seq // tq
    bw = hp * ATTN_DV
    kb0 = OFF_K // bw
    vb0 = OFF_V // bw
    vec = pl.BlockSpec((1, ATTN_DH), lambda b, h, i: (0, 0))
    return pl.pallas_call(
        functools.partial(_attn_kernel, tq=tq, tk=tk, gw=gw, hp=hp),
        grid=(bsz, ATTN_HEADS // hp, nq),
        in_specs=[
            vec, vec, vec, vec,
            pl.BlockSpec((tq, bw), lambda b, h, i: (b * nq + i, h)),
            pl.BlockSpec((seq, bw), lambda b, h, i: (b, kb0 + h)),
            pl.BlockSpec((seq, bw), lambda b, h, i: (b, vb0 + h)),
            pl.BlockSpec((ATTN_DV, 1), lambda b, h, i: (0, 0)),
        ],
        out_specs=pl.BlockSpec((tq, bw), lambda b, h, i: (b * nq + i, h)),
        out_shape=jax.ShapeDtypeStruct((bsz * seq, ATTN_WIDTH), BF16),
        scratch_shapes=[pltpu.VMEM((bw, seq), BF16)],
        compiler_params=pltpu.CompilerParams(
            dimension_semantics=("parallel", "parallel", "arbitrary"), vmem_limit_bytes=VMEM_LIMIT),
        name="diff_attention",
    )(lq1, lk1, lq2, lk2, qkv, qkv, qkv, subln_w)


def _ssd_kernel(z_ref, xs_ref, b_ref, c_ref, dt_ref, cw_ref, cb_ref, dtb_ref, alog_ref,
                dskip_ref, nw_ref, e_ref, o_ref, ext_ref, state_ref):
    ci = pl.program_id(1)
    q = CHUNK
    gw = SSD_WIDTH // SSD_GROUPS

    @pl.when(ci == 0)
    def _init():
        ext_ref[0:SUBLANES, :] = jnp.zeros((SUBLANES, CONV_DIM), F32)
        state_ref[...] = jnp.zeros_like(state_ref)

    u = jnp.concatenate([xs_ref[...], b_ref[...], c_ref[...]], axis=1)
    tail = ext_ref[0:SUBLANES, :]
    row8 = lax.broadcasted_iota(I32, (SUBLANES, CONV_DIM), 0)
    acc = cb_ref[...] + u * cw_ref[CONV_W - 1:CONV_W, :]
    for w in range(CONV_W - 1):
        sh = CONV_W - 1 - w
        rolled = pltpu.roll(u, sh, 0)
        head = jnp.where(row8 < sh, pltpu.roll(tail, sh, 0), rolled[0:SUBLANES, :])
        acc = acc + jnp.concatenate([head, rolled[SUBLANES:, :]], axis=0) * cw_ref[w:w + 1, :]
    ext_ref[0:SUBLANES, :] = u[q - SUBLANES:q, :]
    xbc = acc * _sigmoid(acc)
    xc = xbc[:, :SSD_WIDTH]
    bm = xbc[:, SSD_WIDTH:SSD_WIDTH + SSD_GROUPS * SSD_STATE]
    cm = xbc[:, SSD_WIDTH + SSD_GROUPS * SSD_STATE:]

    lane = lax.broadcasted_iota(I32, (q, LANES), 1)
    head_ok = lane < SSD_HEADS
    dt = jnp.where(head_ok, _softplus(jnp.where(head_ok, dt_ref[...], 0.0) + dtb_ref[...]), 0.0)
    da = dt * (-jnp.exp(alog_ref[...]))
    row = lax.broadcasted_iota(I32, (q, q), 0)
    col = lax.broadcasted_iota(I32, (q, q), 1)
    causal = row >= col
    cs = jnp.dot(causal.astype(F32), da, preferred_element_type=F32,
                 precision=lax.Precision.HIGHEST)
    cs_t = cs.T
    cs_last = cs[q - 1:q, :]

    stk = jnp.concatenate([dt, jnp.exp(cs), jnp.exp(cs_last - cs)], axis=0)
    hi = stk.astype(BF16)
    lo = (stk - hi.astype(F32)).astype(BF16)
    ex = (jnp.dot(hi, e_ref[...], preferred_element_type=F32)
          + jnp.dot(lo, e_ref[...], preferred_element_type=F32))
    dtx = ex[0:q]
    ecx = ex[q:2 * q]
    dsx = ex[2 * q:3 * q]

    xdt = xc * dtx
    xdt_b = xdt.astype(BF16)
    xw_b = (xdt * dsx).astype(BF16)
    lane_q = lax.broadcasted_iota(I32, (q, LANES), 1)

    y_groups = []
    for g in range(SSD_GROUPS):
        gs = slice(g * gw, (g + 1) * gw)
        bg = bm[:, g * SSD_STATE:(g + 1) * SSD_STATE]
        cg_b = cm[:, g * SSD_STATE:(g + 1) * SSD_STATE].astype(BF16)
        cb = lax.dot_general(cg_b, bg.astype(BF16), (((1,), (1,)), ((), ())),
                             preferred_element_type=F32)
        prev_t = state_ref[g]
        y_off = jnp.dot(cg_b, prev_t.astype(BF16), preferred_element_type=F32) * ecx[:, gs]
        st_t = jnp.dot(bg.T.astype(BF16), xw_b[:, gs], preferred_element_type=F32)
        state_ref[g] = prev_t * ecx[q - 1:q, gs] + st_t

        pairs = []
        for jp in range(gw // LANES):
            rhs = xdt_b[:, g * gw + jp * LANES: g * gw + (jp + 1) * LANES]
            outs = []
            for hh in range(2):
                h = g * (SSD_HEADS // SSD_GROUPS) + jp * 2 + hh
                seg = (jnp.broadcast_to(cs[:, h:h + 1], (q, q))
                       - jnp.broadcast_to(cs_t[h:h + 1, :], (q, q)))
                dec = jnp.exp(jnp.where(causal, seg, -jnp.inf))
                outs.append(jnp.dot((cb * dec).astype(BF16), rhs, preferred_element_type=F32))
            pairs.append(jnp.where(lane_q < SSD_HEADDIM, outs[0], outs[1]))
        y_groups.append(jnp.concatenate(pairs, axis=1) + y_off)

    y = jnp.concatenate(y_groups, axis=1) + dskip_ref[...] * xc
    z = z_ref[...]
    y = y * (z * _sigmoid(z))
    outs = []
    for g in range(SSD_GROUPS):
        yg = y[:, g * gw:(g + 1) * gw]
        outs.append(yg * lax.rsqrt(jnp.mean(yg * yg, axis=-1, keepdims=True) + EPS))
    o_ref[...] = (jnp.concatenate(outs, axis=1) * nw_ref[...]).astype(o_ref.dtype)


def _ssd(proj, conv_w, conv_b, dt_bias_p, a_log_p, dskip_x, ssd_norm_w, expand, *, bsz, seq):
    nc = seq // CHUNK
    sw = SSD_GROUPS * SSD_STATE
    const = lambda shape: pl.BlockSpec(shape, lambda b, c: (0, 0))
    return pl.pallas_call(
        _ssd_kernel,
        grid=(bsz, nc),
        in_specs=[
            pl.BlockSpec((CHUNK, SSD_WIDTH), lambda b, c: (b * nc + c, OFF_Z // SSD_WIDTH)),
            pl.BlockSpec((CHUNK, SSD_WIDTH), lambda b, c: (b * nc + c, OFF_X // SSD_WIDTH)),
            pl.BlockSpec((CHUNK, sw), lambda b, c: (b * nc + c, OFF_B // sw)),
            pl.BlockSpec((CHUNK, sw), lambda b, c: (b * nc + c, OFF_C // sw)),
            pl.BlockSpec((CHUNK, LANES), lambda b, c: (b * nc + c, OFF_DT // LANES)),
            const((CONV_W, CONV_DIM)),
            const((1, CONV_DIM)),
            const((1, LANES)),
            const((1, LANES)),
            const((1, SSD_WIDTH)),
            const((1, SSD_WIDTH)),
            const((LANES, SSD_WIDTH)),
        ],
        out_specs=pl.BlockSpec((CHUNK, SSD_WIDTH), lambda b, c: (b * nc + c, 0)),
        out_shape=jax.ShapeDtypeStruct((bsz * seq, SSD_WIDTH), BF16),
        scratch_shapes=[
            pltpu.VMEM((SUBLANES, CONV_DIM), F32),
            pltpu.VMEM((SSD_GROUPS, SSD_STATE, SSD_WIDTH // SSD_GROUPS), F32),
        ],
        compiler_params=pltpu.CompilerParams(
            dimension_semantics=("parallel", "arbitrary"), vmem_limit_bytes=VMEM_LIMIT),
        name="ssd",
    )(proj, proj, proj, proj, proj, conv_w, conv_b, dt_bias_p, a_log_p, dskip_x, ssd_norm_w, expand)


def _out_proj_kernel(x_ref, a_ref, s_ref, wo_ref, n2_ref, rw_ref, rb_ref, x2_ref, h2_ref, lg_ref,
                     *, sub):
    for r0 in range(0, x_ref.shape[0], sub):
        rs = slice(r0, r0 + sub)
        cat = jnp.concatenate([a_ref[rs, :], s_ref[rs, :]], axis=1)
        x2 = x_ref[rs, :] + jnp.dot(cat, wo_ref[...], preferred_element_type=F32)
        x2_ref[rs, :] = x2
        ms = jnp.mean(x2 * x2, axis=-1, keepdims=True)
        h2 = x2 * lax.rsqrt(ms + EPS) * n2_ref[...]
        h2_ref[rs, :] = h2
        h_hi = h2.astype(BF16)
        h_mid = (h2 - h_hi.astype(F32)).astype(BF16)
        p1 = jnp.dot(h_hi, rw_ref[...], preferred_element_type=F32)
        p2 = jnp.dot(h_mid, rw_ref[:, 0:LANES], preferred_element_type=F32)
        lg = p1[:, 0:LANES] + p1[:, LANES:] + p2
        lg_ref[:, rs] = lg.T[0:N_EXPERTS, :] + rb_ref[...]


def _out_proj(x2d, attn, ssd, w_out_b, norm2_w, router_w2, router_b, *, tm, sub):
    t = x2d.shape[0]
    const = lambda shape: pl.BlockSpec(shape, lambda i: (0, 0))
    return pl.pallas_call(
        functools.partial(_out_proj_kernel, sub=sub),
        grid=(t // tm,),
        in_specs=[
            pl.BlockSpec((tm, D_MODEL), lambda i: (i, 0)),
            pl.BlockSpec((tm, ATTN_WIDTH), lambda i: (i, 0)),
            pl.BlockSpec((tm, SSD_WIDTH), lambda i: (i, 0)),
            const((D_MODEL, D_MODEL)),
            const((1, D_MODEL)),
            const((D_MODEL, 2 * LANES)),
            const((N_EXPERTS, 1)),
        ],
        out_specs=[
            pl.BlockSpec((tm, D_MODEL), lambda i: (i, 0)),
            pl.BlockSpec((tm, D_MODEL), lambda i: (i, 0)),
            pl.BlockSpec((N_EXPERTS, tm), lambda i: (0, i)),
        ],
        out_shape=[
            jax.ShapeDtypeStruct((t, D_MODEL), F32),
            jax.ShapeDtypeStruct((t, D_MODEL), F32),
            jax.ShapeDtypeStruct((N_EXPERTS, t), F32),
        ],
        compiler_params=pltpu.CompilerParams(
            dimension_semantics=("parallel",), vmem_limit_bytes=VMEM_LIMIT),
        name="out_proj",
    )(x2d, attn, ssd, w_out_b, norm2_w, router_w2, router_b)


def _route_kernel(lg_ref, idx_ref, gate_ref, pos_ref, cnt_ref, off_ref, rank_ref, *, tt, row_align):
    t_total = lg_ref.shape[1]
    nt = t_total // tt
    eio = lax.broadcasted_iota(I32, (N_EXPERTS, tt), 0)
    r = lax.broadcasted_iota(I32, (tt, tt), 0)
    c = lax.broadcasted_iota(I32, (tt, tt), 1)
    before = jnp.where(r < c, 1.0, 0.0).astype(BF16)

    def tile(ti, cnt):
        off = pl.multiple_of(ti * tt, tt)
        work = lg_ref[:, pl.ds(off, tt)]
        vals, hots = [], []
        for k in range(TOP_K):
            m = jnp.max(work, axis=0, keepdims=True)
            ik = jnp.min(jnp.where(work == m, eio, N_EXPERTS), axis=0, keepdims=True)
            hot = eio == ik
            work = jnp.where(hot, -jnp.inf, work)
            idx_ref[k:k + 1, pl.ds(off, tt)] = ik
            vals.append(m)
            hots.append(hot)
        es = [jnp.exp(v - vals[0]) for v in vals]
        inv = 1.0 / (es[0] + es[1] + es[2] + es[3])
        sel = jnp.zeros((N_EXPERTS, tt), F32)
        for k in range(TOP_K):
            gate_ref[k:k + 1, pl.ds(off, tt)] = es[k] * inv
            sel = sel + jnp.where(hots[k], 1.0, 0.0)
        rank = jnp.dot(sel.astype(BF16), before, preferred_element_type=F32) + cnt
        for k in range(TOP_K):
            rank_ref[k:k + 1, pl.ds(off, tt)] = jnp.sum(jnp.where(hots[k], rank, 0.0), axis=0, keepdims=True)
        return cnt + jnp.sum(sel, axis=1, keepdims=True)

    cnt = lax.fori_loop(0, nt, tile, jnp.zeros((N_EXPERTS, 1), F32))
    cnt_ref[...] = cnt.astype(I32)
    padded = jnp.ceil(cnt * (1.0 / row_align)) * row_align
    er = lax.broadcasted_iota(I32, (N_EXPERTS, N_EXPERTS), 0)
    ec = lax.broadcasted_iota(I32, (N_EXPERTS, N_EXPERTS), 1)
    lower = jnp.where(ec < er, 1.0, 0.0)
    offs = jnp.dot(lower, jnp.broadcast_to(padded, (N_EXPERTS, LANES)), preferred_element_type=F32,
                   precision=lax.Precision.HIGHEST)[:, 0:1]
    off_ref[...] = offs.astype(I32)

    def place(ti, carry):
        off = pl.multiple_of(ti * tt, tt)
        for k in range(TOP_K):
            hot = eio == idx_ref[k:k + 1, pl.ds(off, tt)]
            base = jnp.sum(jnp.where(hot, offs, 0.0), axis=0, keepdims=True)
            pos_ref[k:k + 1, pl.ds(off, tt)] = (base + rank_ref[k:k + 1, pl.ds(off, tt)]).astype(I32)
        return carry

    lax.fori_loop(0, nt, place, 0)


def _route(logits_t, *, tt, row_align):
    t = logits_t.shape[1]
    full = lambda shape: pl.BlockSpec(shape, lambda i: (0, 0))
    return pl.pallas_call(
        functools.partial(_route_kernel, tt=tt, row_align=row_align),
        grid=(1,),
        in_specs=[full((N_EXPERTS, t))],
        out_specs=[full((TOP_K, t)), full((TOP_K, t)), full((TOP_K, t)),
                   full((N_EXPERTS, 1)), full((N_EXPERTS, 1))],
        out_shape=[
            jax.ShapeDtypeStruct((TOP_K, t), I32),
            jax.ShapeDtypeStruct((TOP_K, t), F32),
            jax.ShapeDtypeStruct((TOP_K, t), I32),
            jax.ShapeDtypeStruct((N_EXPERTS, 1), I32),
            jax.ShapeDtypeStruct((N_EXPERTS, 1), I32),
        ],
        scratch_shapes=[pltpu.VMEM((TOP_K, t), F32)],
        compiler_params=pltpu.CompilerParams(
            dimension_semantics=("arbitrary",), vmem_limit_bytes=VMEM_LIMIT),
        name="route",
    )(logits_t)


def _dispatch_kernel(pos_ref, h_ref, xs_ref, sem, *, tm, t_total):
    i = pl.program_id(0)

    def issue(tb, carry):
        for tu in range(ISSUE_UNROLL):
            t = pl.multiple_of(tb * ISSUE_UNROLL, ISSUE_UNROLL) + tu
            for k in range(TOP_K):
                p = pos_ref[k * t_total + i * tm + t]
                pltpu.make_async_copy(h_ref.at[pl.ds(t, 1), :], xs_ref.at[pl.ds(p, 1), :], sem).start()
        return carry

    lax.fori_loop(0, tm // ISSUE_UNROLL, issue, 0)
    for k in range(TOP_K):
        pltpu.make_async_copy(h_ref, xs_ref.at[pl.ds(0, tm), :], sem).wait()


def _dispatch(pos_flat, h2, *, tm, rows_alloc):
    t = h2.shape[0]
    return pl.pallas_call(
        functools.partial(_dispatch_kernel, tm=tm, t_total=t),
        grid_spec=pltpu.PrefetchScalarGridSpec(
            num_scalar_prefetch=1,
            grid=(t // tm,),
            in_specs=[pl.BlockSpec((tm, D_MODEL), lambda i, pos: (i, 0))],
            out_specs=pl.BlockSpec(memory_space=pl.ANY),
            scratch_shapes=[pltpu.SemaphoreType.DMA(())],
        ),
        out_shape=jax.ShapeDtypeStruct((rows_alloc, D_MODEL), F32),
        compiler_params=pltpu.CompilerParams(
            dimension_semantics=("arbitrary",), vmem_limit_bytes=VMEM_LIMIT),
        name="dispatch",
    )(pos_flat, h2)


def _experts_kernel(se_ref, ss_ref, sr_ref, sv_ref, xs_ref, wg_ref, wu_ref, wd_ref, bg_ref, bu_ref,
                    bd_ref, ys_ref, xseg, yacc, sem_in, sem_out, *, unit, max_units, nj):
    s = pl.program_id(0)
    j = pl.program_id(1)
    nseg = pl.num_programs(0)
    rows = sr_ref[s]
    start = ss_ref[s]
    sliver = unit // 4
    ahead = 4

    def plan(r):
        nu = (r + unit - 1) // unit
        return nu, (nu % 2 == 1) & (nu >= 3) & (r - (nu - 1) * unit <= sliver)

    nunits, fuse = plan(rows)
    odd = nunits % 2 == 1
    nbig = nunits // 2 - fuse.astype(I32)
    prev_units, prev_fuse = plan(sr_ref[jnp.maximum(s - 1, 0)])
    prev_units = jnp.where(s > 0, prev_units, 0)
    nxt = jnp.minimum(s + 1, nseg - 1)
    has_next = (s + 1 < nseg) & (sr_ref[nxt] > 0)
    next_units = jnp.where(has_next, plan(sr_ref[nxt])[0], 0)

    def x_copy(u, base):
        r0 = pl.multiple_of(base + u * unit, SUBLANES)
        l0 = pl.multiple_of(u * unit, unit)
        return pltpu.make_async_copy(xs_ref.at[pl.ds(r0, unit), :], xseg.at[pl.ds(l0, unit), :],
                                     sem_in.at[u])

    def y_copy(u, nrows=unit):
        r0 = pl.multiple_of(start + u * unit, SUBLANES)
        l0 = pl.multiple_of(u * unit, unit)
        return pltpu.make_async_copy(yacc.at[pl.ds(l0, nrows), :], ys_ref.at[pl.ds(r0, nrows), :],
                                     sem_out.at[u])

    def fetch(u):
        @pl.when((u < nunits) & (u >= prev_units))
        def _():
            x_copy(u, start).start()

    def await_copy_out(u, n_units, fused):
        @pl.when((u < n_units) & jnp.logical_not(fused & (u == n_units - 1)))
        def _whole():
            y_copy(u).wait()

        @pl.when((u < n_units) & fused & (u == n_units - 1))
        def _part():
            y_copy(u, sliver).wait()

    def chunk(u0, m):
        nu = -(-m // unit)
        l0 = pl.multiple_of(u0 * unit, unit)

        @pl.when(j == 0)
        def _arrived():
            if m == 2 * unit:
                for k in range(2):
                    fetch(u0 + ahead + k)
            for k in range(nu):
                x_copy(u0 + k, start).wait()
                await_copy_out(u0 + k, prev_units, prev_fuse)

        x = xseg[pl.ds(l0, m), :].astype(BF16)
        g = jnp.dot(x, wg_ref[...].astype(BF16), preferred_element_type=F32) + bg_ref[...]
        u = jnp.dot(x, wu_ref[...].astype(BF16), preferred_element_type=F32) + bu_ref[...]
        g = jnp.minimum(g, SWIGLU_LIMIT)
        u = jnp.clip(u, -SWIGLU_LIMIT, SWIGLU_LIMIT)
        act = (g * _sigmoid(SWIGLU_ALPHA * g) * (u + 1.0)).astype(BF16)

        prev = yacc[pl.ds(l0, m), :]
        base = jnp.where(j == 0, jnp.broadcast_to(bd_ref[...], prev.shape), prev)
        yacc[pl.ds(l0, m), :] = base + jnp.dot(
            act, wd_ref[...].astype(BF16), preferred_element_type=F32)

        @pl.when(j == nj - 1)
        def _flush():
            for k in range(m // unit):
                y_copy(u0 + k).start()
            if m % unit:
                y_copy(u0 + m // unit, m % unit).start()
            for k in range(nu):
                @pl.when(u0 + k < next_units)
                def _():
                    x_copy(u0 + k, ss_ref[nxt]).start()

    @pl.when(rows > 0)
    def _work():
        @pl.when((s == 0) & (j == 0))
        def _clear():
            yacc[...] = jnp.zeros_like(yacc)

        @pl.when(j == 0)
        def _begin():
            for u in range(max_units):
                @pl.when(u >= nunits)
                def _():
                    await_copy_out(u, prev_units, prev_fuse)

            for u in range(ahead):
                fetch(u)

        def big(c, carry):
            chunk(2 * c, 2 * unit)
            return carry

        lax.fori_loop(0, nbig, big, 0)

        @pl.when(fuse)
        def _with_sliver():
            chunk(2 * nbig, 2 * unit + sliver)

        @pl.when(odd & jnp.logical_not(fuse))
        def _tail():
            chunk(2 * nbig, unit)

        @pl.when((j == nj - 1) & jnp.logical_not(has_next))
        def _final():
            for u in range(max_units):
                await_copy_out(u, nunits, fuse)


def _experts(seg_e, seg_start, seg_rows, seg_valid, xs, w_gate_up, b_gate_up, w_down, b_down,
             *, unit, tf, cap, nseg):
    nj = D_FF // tf
    rows_alloc = xs.shape[0]

    def jsel(j, sv, s):
        return j * sv[s] + (nj - 1) * (1 - sv[s])

    return pl.pallas_call(
        functools.partial(_experts_kernel, unit=unit, max_units=cap // unit, nj=nj),
        grid_spec=pltpu.PrefetchScalarGridSpec(
            num_scalar_prefetch=4,
            grid=(nseg, nj),
            in_specs=[
                pl.BlockSpec(memory_space=pl.ANY),
                pl.BlockSpec((None, D_MODEL, tf), lambda s, j, se, ss, sr, sv: (se[s], 0, jsel(j, sv, s))),
                pl.BlockSpec((None, D_MODEL, tf), lambda s, j, se, ss, sr, sv: (se[s], 0, nj + jsel(j, sv, s))),
                pl.BlockSpec((None, tf, D_MODEL), lambda s, j, se, ss, sr, sv: (se[s], jsel(j, sv, s), 0)),
                pl.BlockSpec((None, 1, tf), lambda s, j, se, ss, sr, sv: (se[s], 0, jsel(j, sv, s))),
                pl.BlockSpec((None, 1, tf), lambda s, j, se, ss, sr, sv: (se[s], 0, nj + jsel(j, sv, s))),
                pl.BlockSpec((None, 1, D_MODEL), lambda s, j, se, ss, sr, sv: (se[s], 0, 0)),
            ],
            out_specs=pl.BlockSpec(memory_space=pl.ANY),
            scratch_shapes=[
                pltpu.VMEM((cap, D_MODEL), F32),
                pltpu.VMEM((cap, D_MODEL), F32),
                pltpu.SemaphoreType.DMA((cap // unit,)),
                pltpu.SemaphoreType.DMA((cap // unit,)),
            ],
        ),
        out_shape=jax.ShapeDtypeStruct((rows_alloc, D_MODEL), F32),
        compiler_params=pltpu.CompilerParams(
            dimension_semantics=("arbitrary", "arbitrary"), vmem_limit_bytes=VMEM_LIMIT),
        name="experts",
    )(seg_e, seg_start, seg_rows, seg_valid, xs, w_gate_up, w_gate_up, w_down,
      b_gate_up, b_gate_up, b_down)


def _combine_kernel(pos_ref, x2_ref, gate_ref, nf_ref, ys_ref, o_ref, buf, sem, *, tm, t_total):
    i = pl.program_id(0)
    n = pl.num_programs(0)

    def gather(tile, slot):
        def issue(tb, carry):
            for tu in range(ISSUE_UNROLL):
                t = pl.multiple_of(tb * ISSUE_UNROLL, ISSUE_UNROLL) + tu
                for k in range(TOP_K):
                    p = pos_ref[k * t_total + tile * tm + t]
                    pltpu.make_async_copy(ys_ref.at[pl.ds(p, 1), :], buf.at[slot, k, pl.ds(t, 1), :],
                                          sem.at[slot]).start()
            return carry

        lax.fori_loop(0, tm // ISSUE_UNROLL, issue, 0)

    @pl.when(i == 0)
    def _first():
        gather(0, 0)

    @pl.when(i + 1 < n)
    def _ahead():
        gather(i + 1, (i + 1) % 2)

    slot = i % 2
    for k in range(TOP_K):
        pltpu.make_async_copy(ys_ref.at[pl.ds(0, tm), :], buf.at[slot, k], sem.at[slot]).wait()

    g = gate_ref[...]
    x = x2_ref[...]
    for k in range(TOP_K):
        x = x + g[:, k:k + 1] * buf[slot, k]
    ms = jnp.mean(x * x, axis=-1, keepdims=True)
    o_ref[...] = x * lax.rsqrt(ms + EPS) * nf_ref[...]


def _combine(pos_flat, x2, gates_t, norm_f_w, ys, *, tm):
    t = x2.shape[0]
    return pl.pallas_call(
        functools.partial(_combine_kernel, tm=tm, t_total=t),
        grid_spec=pltpu.PrefetchScalarGridSpec(
            num_scalar_prefetch=1,
            grid=(t // tm,),
            in_specs=[
                pl.BlockSpec((tm, D_MODEL), lambda i, pos: (i, 0)),
                pl.BlockSpec((tm, TOP_K), lambda i, pos: (i, 0)),
                pl.BlockSpec((1, D_MODEL), lambda i, pos: (0, 0)),
                pl.BlockSpec(memory_space=pl.ANY),
            ],
            out_specs=pl.BlockSpec((tm, D_MODEL), lambda i, pos: (i, 0)),
            scratch_shapes=[pltpu.VMEM((2, TOP_K, tm, D_MODEL), F32), pltpu.SemaphoreType.DMA((2,))],
        ),
        out_shape=jax.ShapeDtypeStruct((t, D_MODEL), F32),
        compiler_params=pltpu.CompilerParams(
            dimension_semantics=("arbitrary",), vmem_limit_bytes=VMEM_LIMIT),
        name="combine",
    )(pos_flat, x2, gates_t, norm_f_w, ys)


def _tiles(t):
    return dict(
        in_tm=min(1024, t), in_tn=512,
        attn_tq=512, attn_tk=256, attn_gw=128, attn_hp=8,
        out_tm=min(512, t), out_sub=256,
        route_tt=min(512, t),
        disp_tm=min(256, t),
        moe_unit=256, moe_tf=512, moe_cap=min(1536, TOP_K * t),
        comb_tm=min(256, t),
    )


def _segments(cnt, offs, *, cap, nseg):
    nseg_e = (cnt + cap - 1) // cap
    ends = jnp.cumsum(nseg_e)
    total = ends[-1]
    s = jnp.arange(nseg, dtype=I32)
    valid = s < total
    s_eff = jnp.minimum(s, jnp.maximum(total - 1, 0))
    e = jnp.minimum(jnp.sum((s_eff[:, None] >= ends[None, :]).astype(I32), axis=1), N_EXPERTS - 1)
    local = s_eff - (ends[e] - nseg_e[e])
    start = offs[e] + local * cap
    rows = jnp.where(valid, jnp.clip(cnt[e] - local * cap, 0, cap), 0)
    return e, start.astype(I32), rows.astype(I32), valid.astype(I32)


def kernel(x, positions, norm1_w, w_in, lambda_q1, lambda_k1, lambda_q2, lambda_k2, attn_subln_w,
           conv_w, conv_b, dt_bias, a_log, d_skip, ssd_norm_w, w_out, norm2_w, router_w, router_b,
           w_gate_up, b_gate_up, w_down, b_down, norm_f_w):
    bsz, seq, _ = x.shape
    t = bsz * seq
    depth = w_in.shape[0]
    assert depth == 1, "single-layer block"
    cfg = _tiles(t)
    layer = 0

    x2d = x.reshape(t, D_MODEL)
    pos2d = positions.reshape(t, 1).astype(I32)
    inv_freq = ROPE_THETA ** (-jnp.arange(0, ROT_DIM, 2, dtype=F32) / ROT_DIM)
    d = jnp.arange(LANES) % ATTN_DH
    invf = jnp.where(d < ROT_DIM, inv_freq[d % ROT_HALF], 0.0).reshape(1, LANES).astype(F32)

    qkv, proj = _in_proj(x2d, pos2d, invf, norm1_w[layer].reshape(1, D_MODEL), w_in[layer].T,
                         tm=cfg["in_tm"], tn=cfg["in_tn"])

    vec = lambda a: a[layer].reshape(1, -1).astype(F32)
    attn = _attention(qkv, vec(lambda_q1), vec(lambda_k1), vec(lambda_q2), vec(lambda_k2),
                      attn_subln_w[layer].astype(F32).reshape(ATTN_DV, 1),
                      bsz=bsz, seq=seq, tq=min(cfg["attn_tq"], seq),
                      tk=min(cfg["attn_tk"], seq), gw=cfg["attn_gw"], hp=cfg["attn_hp"])

    pad_heads = lambda a: jnp.pad(a[layer].astype(F32), (0, LANES - SSD_HEADS)).reshape(1, LANES)
    head_of_lane = jnp.arange(SSD_WIDTH) // SSD_HEADDIM
    expand = (jnp.arange(LANES)[:, None] == head_of_lane[None, :]).astype(BF16)
    dskip_x = jnp.repeat(d_skip[layer].astype(F32), SSD_HEADDIM).reshape(1, SSD_WIDTH)
    ssd = _ssd(proj, conv_w[layer], conv_b[layer].reshape(1, CONV_DIM), pad_heads(dt_bias),
               pad_heads(a_log), dskip_x, ssd_norm_w[layer].reshape(1, SSD_WIDTH), expand,
               bsz=bsz, seq=seq)

    rw = router_w[layer].astype(F32)
    rw_hi = rw.astype(BF16)
    rw_mid = (rw - rw_hi.astype(F32)).astype(BF16)
    lane_pad = lambda a: jnp.pad(a, ((0, 0), (0, LANES - N_EXPERTS)))
    router_w2 = jnp.concatenate([lane_pad(rw_hi), lane_pad(rw_mid)], axis=1)
    x2, h2, logits_t = _out_proj(
        x2d, attn, ssd, w_out[layer].astype(BF16), norm2_w[layer].reshape(1, D_MODEL),
        router_w2, router_b[layer].reshape(N_EXPERTS, 1),
        tm=cfg["out_tm"], sub=min(cfg["out_sub"], cfg["out_tm"]))

    idx, gates, pos, cnt, offs = _route(logits_t, tt=cfg["route_tt"], row_align=SUBLANES)
    del idx
    pos_flat = pos.reshape(TOP_K * t)

    cap, unit = cfg["moe_cap"], cfg["moe_unit"]
    nseg = N_EXPERTS + (TOP_K * t) // cap
    rows_alloc = TOP_K * t + N_EXPERTS * SUBLANES + unit
    seg_e, seg_start, seg_rows, seg_valid = _segments(
        cnt.reshape(N_EXPERTS), offs.reshape(N_EXPERTS), cap=cap, nseg=nseg)

    xs = _dispatch(pos_flat, h2, tm=cfg["disp_tm"], rows_alloc=rows_alloc)
    ys = _experts(seg_e, seg_start, seg_rows, seg_valid, xs, w_gate_up[layer],
                  b_gate_up[layer].reshape(N_EXPERTS, 1, 2 * D_FF), w_down[layer],
                  b_down[layer].reshape(N_EXPERTS, 1, D_MODEL),
                  unit=unit, tf=cfg["moe_tf"], cap=cap, nseg=nseg)
    out = _combine(pos_flat, x2, gates.T, norm_f_w.reshape(1, D_MODEL), ys, tm=cfg["comb_tm"])
    return out.reshape(bsz, seq, D_MODEL)
```
